```python
import math
import jax, jax.numpy as jnp
from jax import lax
import numpy as np

D_MODEL = 2048
BATCH = 2
SEQ = 8192
DEPTH = 2
DEC_BATCH = 16
DEC_SEQ = 64
PAST_LEN = 2048

CHUNK = 64
Q_BLOCK = 128
EPS = 1e-6

H_A = 4
DK_A = 128
DV_A = 128
H_B = 4
DK_B = 64
DV_B = 128
RET_THETA_BASE = 10000.0
H_C = 4
DH_C = 128
DV_C = 256
D_FF = 4 * D_MODEL

W_A = H_A * DV_A
W_B = H_B * DV_B
W_C = H_C * DV_C
MIX_WIDTH = W_A + W_B + W_C

IN_SIZES = (H_A * DK_A, H_A * DK_A, H_A * DV_A, H_A * DV_A,
            H_B * DK_B, H_B * DK_B, H_B * DV_B, H_B * DV_B,
            H_C * 2 * DH_C, H_C * 2 * DH_C, H_C * DV_C)
IN_WIDTH = sum(IN_SIZES)

kernel_name = "hybrid_hgrn2_retnet_diffattn_stream_step"

F32 = jnp.float32


def rms_unit(x):
    xf = x.astype(F32)
    return xf * lax.rsqrt(jnp.mean(xf * xf, axis=-1, keepdims=True) + EPS)


def rmsnorm(x, g):
    return (rms_unit(x) * g.astype(F32)).astype(x.dtype)


def split_projection(p):
    points = [int(v) for v in np.cumsum(IN_SIZES)[:-1]]
    return jnp.split(p, points, axis=-1)


def to_chunks(t, chunk):
    b, l, h, d = t.shape
    return jnp.moveaxis(t.reshape(b, l // chunk, chunk, h, d), 1, 0)


def from_chunks(t):
    n, b, c, h, d = t.shape
    return jnp.moveaxis(t, 0, 1).reshape(b, n * c, h, d)


def gla_chunked(q, k, v, logf, s0, chunk):
    causal = jnp.tril(jnp.ones((chunk, chunk), bool))

    def step(s, inp):
        qi, ki, vi, gi = inp
        bcum = jnp.cumsum(gi, axis=1)
        diff = bcum[:, :, None] - bcum[:, None, :]
        decay = jnp.exp(jnp.where(causal[None, :, :, None, None], diff, -jnp.inf))
        scores = jnp.einsum('bthk,btshk,bshk->bhts', qi, decay, ki)
        o_intra = jnp.einsum('bhts,bshv->bthv', scores, vi)
        o_inter = jnp.einsum('bthk,bhkv->bthv', qi * jnp.exp(bcum), s)
        btot = bcum[:, -1]
        kdec = ki * jnp.exp(btot[:, None] - bcum)
        s_new = jnp.exp(btot)[..., None] * s + jnp.einsum('bshk,bshv->bhkv', kdec, vi)
        return s_new, o_intra + o_inter

    xs = tuple(to_chunks(t.astype(F32), chunk) for t in (q, k, v, logf))
    s_fin, o = lax.scan(step, s0.astype(F32), xs)
    return from_chunks(o), s_fin


def retention_chunked(q, k, v, log_gamma, s0, chunk):
    idx = jnp.arange(chunk, dtype=F32)
    rel = idx[:, None] - idx[None, :]
    dmat = jnp.where(rel[None] >= 0,
                     jnp.exp(jnp.maximum(rel, 0.0)[None] * log_gamma[:, None, None]), 0.0)
    inner = jnp.exp((idx + 1.0)[None, :] * log_gamma[:, None]).T[None, :, :, None]
    tail = jnp.exp((chunk - 1.0 - idx)[None, :] * log_gamma[:, None])
    full = jnp.exp(chunk * log_gamma)[None, :, None, None]

    def step(s, inp):
        qi, ki, vi = inp
        scores = jnp.einsum('bthk,bshk->bhts', qi, ki) * dmat[None]
        o_intra = jnp.einsum('bhts,bshv->bthv', scores, vi)
        o_inter = jnp.einsum('bthk,bhkv->bthv', qi, s) * inner
        s_new = full * s + jnp.einsum('bshk,hs,bshv->bhkv', ki, tail, vi)
        return s_new, o_intra + o_inter

    xs = tuple(to_chunks(t.astype(F32), chunk) for t in (q, k, v))
    s_fin, o = lax.scan(step, s0.astype(F32), xs)
    return from_chunks(o), s_fin


def rotate(x, pos):
    half = x.shape[-1] // 2
    inv = 1.0 / (RET_THETA_BASE ** jnp.linspace(0.0, 1.0, half, dtype=F32))
    ang = pos[:, None] * inv[None]
    cos = jnp.cos(ang)[None, :, None]
    sin = jnp.sin(ang)[None, :, None]
    x1, x2 = x[..., :half], x[..., half:]
    return jnp.concatenate([x1 * cos - x2 * sin, x1 * sin + x2 * cos], axis=-1)


def diff_attention_core(q, k, v, lam, mask):
    logits = jnp.einsum('bqhcd,bkhcd->bhcqk', q, k).astype(F32) * (DH_C ** -0.5)
    if mask is not None:
        logits = jnp.where(mask, logits, -jnp.inf)
    p = jax.nn.softmax(logits, axis=-1)
    w = p[:, :, 0] - lam * p[:, :, 1]
    return jnp.einsum('bhqk,bkhv->bqhv', w, v.astype(F32))


def diff_attention_prompt(q, k, v, lam):
    b, s = q.shape[0], q.shape[1]
    nblk = s // Q_BLOCK
    kchunk = jnp.arange(s) // CHUNK
    qb = jnp.moveaxis(q.reshape(b, nblk, Q_BLOCK, H_C, 2, DH_C), 1, 0)

    def block(args):
        qi, bi = args
        qchunk = (bi * Q_BLOCK + jnp.arange(Q_BLOCK)) // CHUNK
        mask = kchunk[None, :] <= qchunk[:, None]
        return diff_attention_core(qi, k, v, lam, mask)

    o = lax.map(block, (qb, jnp.arange(nblk)))
    return jnp.moveaxis(o, 0, 1).reshape(b, s, H_C, DV_C)


def trunk_layer(x, pos, s_a, s_b, past_k, past_v, chunk, first_layer, lb, lam, lam_init,
                w_in, w_out, w_up, w_down, g_pre_mix, g_post_mix, g_pre_mlp, g_post_mlp,
                g_hgrn_out, g_diff_out):
    b, L, _ = x.shape
    h = rmsnorm(x, g_pre_mix)
    proj = jnp.einsum('bld,de->ble', h, w_in)
    a_q, a_f, a_i, a_g, b_q, b_k, b_v, b_g, c_q, c_k, c_v = split_projection(proj)

    def heads(t, d):
        return t.astype(F32).reshape(b, L, -1, d)

    a_z = a_f.astype(F32)
    if first_layer:
        a_logf = jax.nn.log_sigmoid(a_z)
    else:
        a_logf = jnp.log(lb + (1.0 - lb) * jax.nn.sigmoid(a_z))
    a_key = -jnp.expm1(a_logf)
    a_qs = jax.nn.silu(a_q.astype(F32)) * (DK_A ** -0.5)
    o_a, s_a_new = gla_chunked(heads(a_qs, DK_A), heads(a_key, DK_A), heads(a_i, DV_A),
                               heads(a_logf, DK_A), s_a, chunk)
    o_a = rms_unit(o_a) * g_hgrn_out.astype(F32) * jax.nn.silu(heads(a_g, DV_A))

    log_gamma = jnp.log1p(-jnp.power(2.0, -5.0 - jnp.arange(H_B, dtype=F32)))
    bq = rotate(heads(b_q, DK_B), pos)
    bk = rotate(heads(b_k, DK_B), pos) * (DK_B ** -0.5)
    o_b, s_b_new = retention_chunked(bq, bk, heads(b_v, DV_B), log_gamma, s_b, chunk)
    o_b = rms_unit(o_b) * jax.nn.silu(heads(b_g, DV_B))

    cq = c_q.reshape(b, L, H_C, 2, DH_C)
    ck = c_k.reshape(b, L, H_C, 2, DH_C)
    cv = c_v.reshape(b, L, H_C, DV_C)
    if past_k is None:
        o_c = diff_attention_prompt(cq, ck, cv, lam)
    else:
        keys = jnp.concatenate([past_k.reshape(b, -1, H_C, 2, DH_C).astype(ck.dtype), ck], axis=1)
        vals = jnp.concatenate([past_v.astype(cv.dtype), cv], axis=1)
        o_c = diff_attention_core(cq, keys, vals, lam, None)
    o_c = rms_unit(o_c) * g_diff_out.astype(F32) * (1.0 - lam_init)

    mix = jnp.concatenate([o_a.reshape(b, L, W_A), o_b.reshape(b, L, W_B),
                           o_c.reshape(b, L, W_C)], axis=-1).astype(x.dtype)
    x = x + rmsnorm(jnp.einsum('ble,ed->bld', mix, w_out), g_post_mix)

    hm = rmsnorm(x, g_pre_mlp)
    u = jnp.square(jax.nn.relu(jnp.einsum('bld,df->blf', hm, w_up)))
    x = x + rmsnorm(jnp.einsum('blf,fd->bld', u, w_down), g_post_mlp)
    return x, s_a_new, s_b_new, ck.reshape(b, L, H_C, 2 * DH_C), cv


def setup_inputs(seed: int = 0) -> dict:
    key = jax.random.key(seed)
    ks = jax.random.split(key, 24)

    def nrm(k, shape, scale):
        return jax.random.normal(k, shape, F32) * scale

    def gain(k, shape):
        return 1.0 + 0.05 * jax.random.normal(k, shape, F32)

    return {
        "x_prompt": nrm(ks[0], (BATCH, SEQ, D_MODEL), 1.0),
        "x_sample": nrm(ks[1], (DEC_BATCH, DEC_SEQ, D_MODEL), 1.0),
        "state_hgrn": nrm(ks[2], (DEPTH, DEC_BATCH, H_A, DK_A, DV_A), 0.5),
        "state_ret": nrm(ks[3], (DEPTH, DEC_BATCH, H_B, DK_B, DV_B), 0.5),
        "cache_k": nrm(ks[4], (DEPTH, DEC_BATCH, PAST_LEN, H_C, 2 * DH_C), 1.0),
        "cache_v": nrm(ks[5], (DEPTH, DEC_BATCH, PAST_LEN, H_C, DV_C), 1.0),
        "w_in": nrm(ks[6], (DEPTH, D_MODEL, IN_WIDTH), D_MODEL ** -0.5),
        "w_out": nrm(ks[7], (DEPTH, MIX_WIDTH, D_MODEL), MIX_WIDTH ** -0.5),
        "w_up": nrm(ks[8], (DEPTH, D_MODEL, D_FF), D_MODEL ** -0.5),
        "w_down": nrm(ks[9], (DEPTH, D_FF, D_MODEL), D_FF ** -0.5),
        "g_pre_mix": gain(ks[10], (DEPTH, D_MODEL)),
        "g_post_mix": gain(ks[11], (DEPTH, D_MODEL)),
        "g_pre_mlp": gain(ks[12], (DEPTH, D_MODEL)),
        "g_post_mlp": gain(ks[13], (DEPTH, D_MODEL)),
        "hgrn_lb_logits": nrm(ks[14], (DEPTH, H_A * DK_A), 0.5),
        "g_hgrn_out": gain(ks[15], (DEPTH, DV_A)),
        "lambda_q1": nrm(ks[16], (DEPTH, DH_C), 0.1),
        "lambda_k1": nrm(ks[17], (DEPTH, DH_C), 0.1),
        "lambda_q2": nrm(ks[18], (DEPTH, DH_C), 0.1),
        "lambda_k2": nrm(ks[19], (DEPTH, DH_C), 0.1),
        "g_diff_out": gain(ks[20], (DEPTH, DV_C)),
    }


def reference(x_prompt, x_sample, state_hgrn, state_ret, cache_k, cache_v,
              w_in, w_out, w_up, w_down, g_pre_mix, g_post_mix, g_pre_mlp, g_post_mlp,
              hgrn_lb_logits, g_hgrn_out, lambda_q1, lambda_k1, lambda_q2, lambda_k2, g_diff_out):
    b_p, seq_p = x_prompt.shape[0], x_prompt.shape[1]
    n_new = x_sample.shape[1]
    past_len = cache_k.shape[2]
    pos_p = jnp.arange(seq_p, dtype=F32)
    pos_s = past_len + jnp.arange(n_new, dtype=F32)

    lb_soft = jax.nn.softmax(hgrn_lb_logits.astype(F32), axis=0)
    lower_bounds = jnp.cumsum(lb_soft, axis=0) - lb_soft[0]

    s_a0 = jnp.zeros((b_p, H_A, DK_A, DV_A), F32)
    s_b0 = jnp.zeros((b_p, H_B, DK_B, DV_B), F32)

    hp, hs = x_prompt, x_sample
    sa_p, sa_s, sb_p, sb_s, k_p, k_s, v_p, v_s = [], [], [], [], [], [], [], []
    for l in range(DEPTH):
        lam_init = 0.8 - 0.6 * math.exp(-0.3 * l)
        lam = (jnp.exp(jnp.sum(lambda_q1[l].astype(F32) * lambda_k1[l].astype(F32)))
               - jnp.exp(jnp.sum(lambda_q2[l].astype(F32) * lambda_k2[l].astype(F32))) + lam_init)
        hp, a1, b1, kr1, vr1 = trunk_layer(
            hp, pos_p, s_a0, s_b0, None, None, CHUNK, l == 0, lower_bounds[l], lam, lam_init,
            w_in[l], w_out[l], w_up[l], w_down[l], g_pre_mix[l], g_post_mix[l], g_pre_mlp[l],
            g_post_mlp[l], g_hgrn_out[l], g_diff_out[l])
        hs, a2, b2, kr2, vr2 = trunk_layer(
            hs, pos_s, state_hgrn[l], state_ret[l], cache_k[l], cache_v[l], n_new, l == 0,
            lower_bounds[l], lam, lam_init,
            w_in[l], w_out[l], w_up[l], w_down[l], g_pre_mix[l], g_post_mix[l], g_pre_mlp[l],
            g_post_mlp[l], g_hgrn_out[l], g_diff_out[l])
        sa_p.append(a1); sb_p.append(b1); k_p.append(kr1); v_p.append(vr1)
        sa_s.append(a2); sb_s.append(b2); k_s.append(kr2); v_s.append(vr2)

    return (hp, hs, jnp.stack(sa_p), jnp.stack(sa_s), jnp.stack(sb_p), jnp.stack(sb_s),
            jnp.stack(k_p), jnp.stack(k_s), jnp.stack(v_p), jnp.stack(v_s))
```

```python
import functools
import math

import jax
import jax.numpy as jnp
from jax import lax
from jax.experimental import pallas as pl
from jax.experimental.pallas import tpu as pltpu

F32 = jnp.float32
BF16 = jnp.bfloat16

D_MODEL = 2048
CHUNK = 64
EPS = 1e-6
H_A, DK_A, DV_A = 4, 128, 128
H_B, DK_B, DV_B = 4, 64, 128
RET_THETA_BASE = 10000.0
H_C, DH_C, DV_C = 4, 128, 256
D_FF = 4 * D_MODEL
W_A = H_A * DV_A
W_B = H_B * DV_B
W_C = H_C * DV_C
W_AB = 4 * H_A * DK_A + 2 * H_B * DK_B + 2 * H_B * DV_B
W_CQ = H_C * 2 * DH_C
IN_WIDTH = W_AB + 3 * W_CQ
HALF_B = DK_B // 2

VMEM_LIMIT_BYTES = 56 * 1024 * 1024
SUBCHUNK = 16
NEG_BIG = -1e30


def _cparams(n_axes):
    return pltpu.CompilerParams(dimension_semantics=("arbitrary",) * n_axes,
                                vmem_limit_bytes=VMEM_LIMIT_BYTES)


def _rms_scale(x):
    return lax.rsqrt(jnp.mean(x * x, axis=-1, keepdims=True) + EPS)


def _sigmoid(x):
    return 1.0 / (1.0 + jnp.exp(-x))


def _dot(a, b):
    return jnp.dot(a, b, preferred_element_type=F32)


def _dot_nt(a, b):
    return lax.dot_general(a, b, (((1,), (1,)), ((), ())), preferred_element_type=F32)


def _dot_tn(a, b):
    return lax.dot_general(a, b, (((0,), (0,)), ((), ())), preferred_element_type=F32)


def _inproj_kernel(x_ref, g_ref, w_ref, ab_ref, q_ref, k32_ref, kbf_ref, v32_ref, vbf_ref, h_scr,
                   *, n_ab, n_c, q_scale):
    j = pl.program_id(1)

    @pl.when(j == 0)
    def _():
        x = x_ref[...]
        h_scr[...] = (x * _rms_scale(x) * g_ref[...]).astype(BF16)

    @pl.when(j < n_ab)
    def _():
        ab_ref[...] = _dot(h_scr[...], w_ref[...])

    @pl.when((j >= n_ab) & (j < n_ab + n_c))
    def _():
        q_ref[...] = (_dot(h_scr[...], w_ref[...]) * q_scale).astype(BF16)

    @pl.when((j >= n_ab + n_c) & (j < n_ab + 2 * n_c))
    def _():
        acc = _dot(h_scr[...], w_ref[...])
        k32_ref[...] = acc
        kbf_ref[...] = acc.astype(BF16)

    @pl.when(j >= n_ab + 2 * n_c)
    def _():
        acc = _dot(h_scr[...], w_ref[...])
        v32_ref[...] = acc
        vbf_ref[...] = acc.astype(BF16)


def _inproj(x, g, w, *, tm, tn):
    t = x.shape[0]
    n_ab, n_c = W_AB // tn, W_CQ // tn
    kern = functools.partial(_inproj_kernel, n_ab=n_ab, n_c=n_c, q_scale=DH_C ** -0.5)

    def cmap(lo):
        return lambda i, j: (i, jnp.clip(j - lo, 0, n_c - 1))

    return pl.pallas_call(
        kern,
        grid=(t // tm, IN_WIDTH // tn),
        in_specs=[pl.BlockSpec((tm, D_MODEL), lambda i, j: (i, 0)),
                  pl.BlockSpec((1, D_MODEL), lambda i, j: (0, 0)),
                  pl.BlockSpec((D_MODEL, tn), lambda i, j: (0, j))],
        out_specs=[pl.BlockSpec((tm, tn), lambda i, j: (i, jnp.minimum(j, n_ab - 1))),
                   pl.BlockSpec((tm, tn), cmap(n_ab)),
                   pl.BlockSpec((tm, tn), cmap(n_ab + n_c)),
                   pl.BlockSpec((tm, tn), cmap(n_ab + n_c)),
                   pl.BlockSpec((tm, tn), cmap(n_ab + 2 * n_c)),
                   pl.BlockSpec((tm, tn), cmap(n_ab + 2 * n_c))],
        out_shape=[jax.ShapeDtypeStruct((t, W_AB), F32),
                   jax.ShapeDtypeStruct((t, W_CQ), BF16),
                   jax.ShapeDtypeStruct((t, W_CQ), F32),
                   jax.ShapeDtypeStruct((t, W_CQ), BF16),
                   jax.ShapeDtypeStruct((t, W_CQ), F32),
                   jax.ShapeDtypeStruct((t, W_CQ), BF16)],
        scratch_shapes=[pltpu.VMEM((tm, D_MODEL), BF16)],
        compiler_params=_cparams(2),
        name="inproj",
    )(x, g, w)


def _hgrn_kernel(lb_ref, gout_ref, s0_ref, q_ref, f_ref, i_ref, g_ref, o_ref, sfin_ref, st_scr,
                 *, first_layer, n_chunks):
    t = pl.program_id(1)

    @pl.when(t == 0)
    def _():
        for h in range(H_A):
            st_scr[h] = s0_ref[h].T

    rows = lax.broadcasted_iota(jnp.int32, (CHUNK, CHUNK), 0)
    cols = lax.broadcasted_iota(jnp.int32, (CHUNK, CHUNK), 1)
    tri = (cols <= rows).astype(BF16)
    sub_rows = lax.broadcasted_iota(jnp.int32, (SUBCHUNK, 1), 0)
    gout = gout_ref[...]
    n_sub = CHUNK // SUBCHUNK

    def chunk_body(c, carry):
        r0 = pl.multiple_of(c * CHUNK, CHUNK)
        zq = q_ref[pl.ds(r0, CHUNK), :]
        zf = f_ref[pl.ds(r0, CHUNK), :]
        vi = i_ref[pl.ds(r0, CHUNK), :]
        zg = g_ref[pl.ds(r0, CHUNK), :]
        lb = lb_ref[...]

        e = jnp.exp(-jnp.abs(zf))
        r = 1.0 / (1.0 + e)
        pos = zf >= 0.0
        if first_layer:
            logf = jnp.minimum(zf, 0.0) - jnp.log(1.0 + e)
        else:
            logf = jnp.log(lb + (1.0 - lb) * jnp.where(pos, r, e * r))
        kk = (1.0 - lb) * jnp.where(pos, e * r, r)
        qs = zq * _sigmoid(zq) * (DK_A ** -0.5)
        gate = zg * _sigmoid(zg)

        g_hi = logf.astype(BF16)
        r1 = logf - g_hi.astype(F32)
        g_mid = r1.astype(BF16)
        g_lo = (r1 - g_mid.astype(F32)).astype(BF16)
        bcum = _dot(tri, g_hi) + _dot(tri, g_mid) + _dot(tri, g_lo)

        for h in range(H_A):
            sl = slice(h * DK_A, (h + 1) * DK_A)
            b = bcum[:, sl]
            q_h = qs[:, sl]
            k_h = kk[:, sl]
            v_h = vi[:, sl]
            v_bf = v_h.astype(BF16)
            st = st_scr[h]
            btot = b[CHUNK - 1:CHUNK, :]

            o_inter = _dot_nt((q_h * jnp.exp(b)).astype(BF16), st.astype(BF16))
            kdec = (k_h * jnp.exp(btot - b)).astype(BF16)
            st_scr[h] = st * jnp.exp(btot) + _dot_tn(v_bf, kdec)

            for i in range(n_sub):
                lo = i * SUBCHUNK
                b_i = b[lo:lo + SUBCHUNK, :]
                q_i = q_h[lo:lo + SUBCHUNK, :]
                o_i = o_inter[lo:lo + SUBCHUNK, :]
                if i > 0:
                    ref_row = b[lo - 1:lo, :]
                    qd = (q_i * jnp.exp(b_i - ref_row)).astype(BF16)
                    kd = (k_h[:lo, :] * jnp.exp(ref_row - b[:lo, :])).astype(BF16)
                    a = _dot_nt(qd, kd)
                    o_i = o_i + _dot(a.astype(BF16), v_bf[:lo, :])
                for s in range(SUBCHUNK):
                    sg = lo + s
                    d = jnp.exp(jnp.minimum(b_i - b[sg:sg + 1, :], 0.0))
                    w = jnp.sum(q_i * d * k_h[sg:sg + 1, :], axis=-1, keepdims=True)
                    w = jnp.where(sub_rows >= s, w, 0.0)
                    o_i = o_i + w * v_h[sg:sg + 1, :]
                y = o_i * _rms_scale(o_i) * gout * gate[lo:lo + SUBCHUNK, sl]
                o_ref[pl.ds(r0 + lo, SUBCHUNK), sl] = y.astype(BF16)
        return carry

    lax.fori_loop(0, n_chunks, chunk_body, 0)

    @pl.when(t == pl.num_programs(1) - 1)
    def _():
        for h in range(H_A):
            sfin_ref[h] = st_scr[h].T


def _hgrn(ab, s0, lb, gout, *, first_layer, tc):
    b, l, _ = ab.shape
    n_chunks = tc // CHUNK
    kern = functools.partial(_hgrn_kernel, first_layer=first_layer, n_chunks=n_chunks)
    blk = H_A * DK_A

    def col(k):
        return pl.BlockSpec((None, tc, blk), lambda bi, ti: (bi, ti, k))

    state_spec = pl.BlockSpec((None, H_A, DK_A, DV_A), lambda bi, ti: (bi, 0, 0, 0))
    return pl.pallas_call(
        kern,
        grid=(b, l // tc),
        in_specs=[pl.BlockSpec((1, blk), lambda bi, ti: (0, 0)),
                  pl.BlockSpec((1, DV_A), lambda bi, ti: (0, 0)),
                  state_spec, col(0), col(1), col(2), col(3)],
        out_specs=[pl.BlockSpec((None, tc, W_A), lambda bi, ti: (bi, ti, 0)), state_spec],
        out_shape=[jax.ShapeDtypeStruct((b, l, W_A), BF16),
                   jax.ShapeDtypeStruct((b, H_A, DK_A, DV_A), F32)],
        scratch_shapes=[pltpu.VMEM((H_A, DV_A, DK_A), F32)],
        compiler_params=_cparams(2),
        name="hgrn",
    )(lb, gout, s0, ab, ab, ab, ab)


_AB_RET_Q = (4 * H_A * DK_A) // (H_B * DK_B)
_AB_RET_V = (4 * H_A * DK_A + 2 * H_B * DK_B) // W_B
_LOG_GAMMA = tuple(math.log1p(-(2.0 ** (-5.0 - h))) for h in range(H_B))


def _ret_kernel(cos_ref, sin_ref, s0_ref, q_ref, k_ref, v_ref, g_ref, o_ref, sfin_ref, s_scr, *, tc):
    t = pl.program_id(1)
    wq = H_B * DK_B
    wh = H_B * HALF_B

    @pl.when(t == 0)
    def _():
        s_scr[...] = jnp.zeros_like(s_scr)
        for h in range(H_B):
            s_scr[h, h * HALF_B:(h + 1) * HALF_B, :] = s0_ref[h, :HALF_B, :]
            s_scr[h, wh + h * HALF_B:wh + (h + 1) * HALF_B, :] = s0_ref[h, HALF_B:, :]

    c = cos_ref[...]
    s = sin_ref[...]

    def rot(x):
        x1, x2 = x[:, :wh], x[:, wh:]
        return jnp.concatenate([x1 * c - x2 * s, x1 * s + x2 * c], axis=1)

    qr = rot(q_ref[...]).astype(BF16)
    kr = rot(k_ref[...]) * (DK_B ** -0.5)
    v = v_ref[...]
    gate_in = g_ref[...]

    lane_head = (lax.broadcasted_iota(jnp.int32, (1, wq), 1) % wh) // HALF_B
    rel = (lax.broadcasted_iota(jnp.int32, (tc, tc), 0)
           - lax.broadcasted_iota(jnp.int32, (tc, tc), 1))
    relf = jnp.maximum(rel, 0).astype(F32)
    tpos = lax.broadcasted_iota(jnp.int32, (tc, 1), 0).astype(F32)

    for h in range(H_B):
        lg = _LOG_GAMMA[h]
        sl = slice(h * DV_B, (h + 1) * DV_B)
        kh = jnp.where(lane_head == h, kr, 0.0)
        v_bf = v[:, sl].astype(BF16)
        dmat = jnp.where(rel >= 0, jnp.exp(relf * lg), 0.0)
        sc = _dot_nt(qr, kh.astype(BF16)) * dmat
        st = s_scr[h]
        o = _dot(sc.astype(BF16), v_bf) + _dot(qr, st.astype(BF16)) * jnp.exp((tpos + 1.0) * lg)
        ktail = (kh * jnp.exp((tc - 1.0 - tpos) * lg)).astype(BF16)
        s_scr[h] = math.exp(tc * lg) * st + _dot_tn(ktail, v_bf)
        zg = gate_in[:, sl]
        o_ref[:, sl] = (o * _rms_scale(o) * (zg * _sigmoid(zg))).astype(BF16)

    @pl.when(t == pl.num_programs(1) - 1)
    def _():
        for h in range(H_B):
            sfin_ref[h, :HALF_B, :] = s_scr[h, h * HALF_B:(h + 1) * HALF_B, :]
            sfin_ref[h, HALF_B:, :] = s_scr[h, wh + h * HALF_B:wh + (h + 1) * HALF_B, :]


def _retention(ab, s0, cos, sin, *, tc):
    b, l, _ = ab.shape
    wq = H_B * DK_B
    wh = H_B * HALF_B
    kern = functools.partial(_ret_kernel, tc=tc)
    state_spec = pl.BlockSpec((None, H_B, DK_B, DV_B), lambda bi, ti: (bi, 0, 0, 0))
    return pl.pallas_call(
        kern,
        grid=(b, l // tc),
        in_specs=[pl.BlockSpec((tc, wh), lambda bi, ti: (ti, 0)),
                  pl.BlockSpec((tc, wh), lambda bi, ti: (ti, 0)),
                  state_spec,
                  pl.BlockSpec((None, tc, wq), lambda bi, ti: (bi, ti, _AB_RET_Q)),
                  pl.BlockSpec((None, tc, wq), lambda bi, ti: (bi, ti, _AB_RET_Q + 1)),
                  pl.BlockSpec((None, tc, W_B), lambda bi, ti: (bi, ti, _AB_RET_V)),
                  pl.BlockSpec((None, tc, W_B), lambda bi, ti: (bi, ti, _AB_RET_V + 1))],
        out_specs=[pl.BlockSpec((None, tc, W_B), lambda bi, ti: (bi, ti, 0)), state_spec],
        out_shape=[jax.ShapeDtypeStruct((b, l, W_B), BF16),
                   jax.ShapeDtypeStruct((b, H_B, DK_B, DV_B), F32)],
        scratch_shapes=[pltpu.VMEM((H_B, wq, DV_B), F32)],
        compiler_params=_cparams(2),
        name="retention",
    )(cos, sin, s0, ab, ab, ab, ab)


def _diff_out(o1, l1, o2, l2, lam, g, out_scale):
    o = o1 / l1 - lam * (o2 / l2)
    return (o * _rms_scale(o) * g * out_scale).astype(BF16)


def _attn_prompt_kernel(lam_ref, q_ref, k_ref, v_ref, g_ref, o_ref, acc_scr, *, blk, out_scale):
    qi = pl.program_id(2)
    q = q_ref[...]
    qh = (q[:, :DH_C], q[:, DH_C:])
    rq = lax.broadcasted_iota(jnp.int32, (blk, blk), 0) // CHUNK
    ck = lax.broadcasted_iota(jnp.int32, (blk, blk), 1) // CHUNK
    visible = ck <= rq
    acc_scr[...] = jnp.zeros_like(acc_scr)

    def block(j, carry, masked):
        off = pl.multiple_of(j * blk, blk)
        kb = k_ref[pl.ds(off, blk), :]
        vb = v_ref[pl.ds(off, blk), :]
        new = []
        for half in range(2):
            m_old, l_old = carry[2 * half], carry[2 * half + 1]
            s = _dot_nt(qh[half], kb[:, half * DH_C:(half + 1) * DH_C])
            if masked:
                s = jnp.where(visible, s, -jnp.inf)
            m_new = jnp.maximum(m_old, jnp.max(s, axis=-1, keepdims=True))
            alpha = jnp.exp(m_old - m_new)
            p = jnp.exp(s - m_new)
            l_new = alpha * l_old + jnp.sum(p, axis=-1, keepdims=True)
            acc_scr[half] = alpha * acc_scr[half] + _dot(p.astype(BF16), vb)
            new += [m_new, l_new]
        return tuple(new)

    m0 = jnp.full((blk, 1), NEG_BIG, F32)
    l0 = jnp.zeros((blk, 1), F32)
    carry = lax.fori_loop(0, qi, lambda j, cr: block(j, cr, False), (m0, l0, m0, l0))
    _, l1, _, l2 = block(qi, carry, True)
    o_ref[...] = _diff_out(acc_scr[0], l1, acc_scr[1], l2, lam_ref[0, 0], g_ref[...], out_scale)


def _attn_prompt(q, k, v, lam, g, *, blk, out_scale):
    b, l, _ = q.shape
    kern = functools.partial(_attn_prompt_kernel, blk=blk, out_scale=out_scale)
    kv_spec = pl.BlockSpec((None, l, DV_C), lambda bi, hi, qi: (bi, 0, hi))
    return pl.pallas_call(
        kern,
        grid=(b, H_C, l // blk),
        in_specs=[pl.BlockSpec(memory_space=pltpu.SMEM),
                  pl.BlockSpec((None, blk, DV_C), lambda bi, hi, qi: (bi, qi, hi)),
                  kv_spec, kv_spec,
                  pl.BlockSpec((1, DV_C), lambda bi, hi, qi: (0, 0))],
        out_specs=pl.BlockSpec((None, blk, DV_C), lambda bi, hi, qi: (bi, qi, hi)),
        out_shape=jax.ShapeDtypeStruct((b, l, W_C), BF16),
        scratch_shapes=[pltpu.VMEM((2, blk, DV_C), F32)],
        compiler_params=_cparams(3),
        name="attn_prompt",
    )(lam, q, k, v, g)


def _attn_sample_kernel(lam_ref, q_ref, kn_ref, vn_ref, kp_ref, vp_ref, g_ref, o_ref, *, out_scale):
    q = q_ref[...]
    kn = kn_ref[...]
    vn = vn_ref[...]
    kp = kp_ref[...].astype(BF16)
    vp = vp_ref[...].astype(BF16)
    outs = []
    for half in range(2):
        hs = slice(half * DH_C, (half + 1) * DH_C)
        s_p = _dot_nt(q[:, hs], kp[:, hs])
        s_n = _dot_nt(q[:, hs], kn[:, hs])
        m = jnp.maximum(jnp.max(s_p, axis=-1, keepdims=True), jnp.max(s_n, axis=-1, keepdims=True))
        p_p = jnp.exp(s_p - m)
        p_n = jnp.exp(s_n - m)
        l = jnp.sum(p_p, axis=-1, keepdims=True) + jnp.sum(p_n, axis=-1, keepdims=True)
        outs += [_dot(p_p.astype(BF16), vp) + _dot(p_n.astype(BF16), vn), l]
    o_ref[...] = _diff_out(outs[0], outs[1], outs[2], outs[3], lam_ref[0, 0], g_ref[...], out_scale)


def _attn_sample(q, kn, vn, cache_k, cache_v, layer, lam, g, *, out_scale):
    b, n, _ = q.shape
    past = cache_k.shape[2]
    kern = functools.partial(_attn_sample_kernel, out_scale=out_scale)
    new_spec = pl.BlockSpec((None, n, DV_C), lambda bi, hi: (bi, 0, hi))
    past_spec = pl.BlockSpec((None, None, past, DV_C), lambda bi, hi: (layer, bi, 0, hi))
    return pl.pallas_call(
        kern,
        grid=(b, H_C),
        in_specs=[pl.BlockSpec(memory_space=pltpu.SMEM),
                  new_spec, new_spec, new_spec, past_spec, past_spec,
                  pl.BlockSpec((1, DV_C), lambda bi, hi: (0, 0))],
        out_specs=new_spec,
        out_shape=jax.ShapeDtypeStruct((b, n, W_C), BF16),
        compiler_params=_cparams(2),
        name="attn_sample",
    )(lam, q, kn, vn, cache_k, cache_v, g)


def _outproj_kernel(oa_ref, ob_ref, oc_ref, wa_ref, wb_ref, wc_ref, x_ref, gpost_ref, gpre_ref,
                    x1_ref, hm_ref):
    y = _dot(oa_ref[...], wa_ref[...]) + _dot(ob_ref[...], wb_ref[...]) + _dot(oc_ref[...], wc_ref[...])
    x1 = x_ref[...] + y * _rms_scale(y) * gpost_ref[...]
    x1_ref[...] = x1
    hm_ref[...] = (x1 * _rms_scale(x1) * gpre_ref[...]).astype(BF16)


def _outproj(oa, ob, oc, w_out, x, gpost, gpre, *, tm):
    t = x.shape[0]
    row = lambda w: pl.BlockSpec((tm, w), lambda i: (i, 0))
    vec = pl.BlockSpec((1, D_MODEL), lambda i: (0, 0))
    return pl.pallas_call(
        _outproj_kernel,
        grid=(t // tm,),
        in_specs=[row(W_A), row(W_B), row(W_C),
                  pl.BlockSpec((W_A, D_MODEL), lambda i: (0, 0)),
                  pl.BlockSpec((W_B, D_MODEL), lambda i: (1, 0)),
                  pl.BlockSpec((W_C, D_MODEL), lambda i: (1, 0)),
                  row(D_MODEL), vec, vec],
        out_specs=[row(D_MODEL), row(D_MODEL)],
        out_shape=[jax.ShapeDtypeStruct((t, D_MODEL), F32),
                   jax.ShapeDtypeStruct((t, D_MODEL), BF16)],
        compiler_params=_cparams(1),
        name="outproj",
    )(oa, ob, oc, w_out, w_out, w_out, x, gpost, gpre)


def _mlp_kernel(hm_ref, wu_ref, wd_ref, x1_ref, g_ref, o_ref, acc_scr):
    f = pl.program_id(1)
    u = jnp.maximum(_dot(hm_ref[...], wu_ref[...]), 0.0)
    contrib = _dot((u * u).astype(BF16), wd_ref[...])

    @pl.when(f == 0)
    def _():
        acc_scr[...] = contrib

    @pl.when(f > 0)
    def _():
        acc_scr[...] += contrib

    @pl.when(f == pl.num_programs(1) - 1)
    def _():
        y = acc_scr[...]
        o_ref[...] = x1_ref[...] + y * _rms_scale(y) * g_ref[...]


def _mlp(hm, w_up, w_down, x1, g, *, tm, tf):
    t = hm.shape[0]
    row = lambda dt: pl.BlockSpec((tm, D_MODEL), lambda i, f: (i, 0))
    return pl.pallas_call(
        _mlp_kernel,
        grid=(t // tm, D_FF // tf),
        in_specs=[row(BF16),
                  pl.BlockSpec((D_MODEL, tf), lambda i, f: (0, f)),
                  pl.BlockSpec((tf, D_MODEL), lambda i, f: (f, 0)),
                  row(F32),
                  pl.BlockSpec((1, D_MODEL), lambda i, f: (0, 0))],
        out_specs=row(F32),
        out_shape=jax.ShapeDtypeStruct((t, D_MODEL), F32),
        scratch_shapes=[pltpu.VMEM((tm, D_MODEL), F32)],
        compiler_params=_cparams(2),
        name="mlp",
    )(hm, w_up, w_down, x1, g)


def _rotary_tables(pos):
    inv = 1.0 / (RET_THETA_BASE ** jnp.linspace(0.0, 1.0, HALF_B, dtype=F32))
    ang = pos[:, None] * inv[None]
    return jnp.tile(jnp.cos(ang), (1, H_B)), jnp.tile(jnp.sin(ang), (1, H_B))


def _prep_w_in(w):
    def regroup(cols):
        return cols.reshape(D_MODEL, H_B, 2, HALF_B).transpose(0, 2, 1, 3).reshape(D_MODEL, H_B * DK_B)

    a_end = 4 * H_A * DK_A
    wq = H_B * DK_B
    parts = [w[:, :a_end], regroup(w[:, a_end:a_end + wq]),
             regroup(w[:, a_end + wq:a_end + 2 * wq]), w[:, a_end + 2 * wq:]]
    return jnp.concatenate(parts, axis=1).astype(BF16)


def _pick(n, pref):
    return pref if n % pref == 0 else n


def _layer(x, batch, s_a, s_b, rot, cache, layer, lb, lam, lam_init, wts):
    w_in, w_out, w_up, w_down, g_pre_mix, g_post_mix, g_pre_mlp, g_post_mlp, g_hgrn, g_diff = wts
    t = x.shape[0]
    l = t // batch
    tm = _pick(t, 512)
    ab, q, k32, kbf, v32, vbf = _inproj(x, g_pre_mix, w_in, tm=tm, tn=512)
    ab3 = ab.reshape(batch, l, W_AB)
    oa, sa_new = _hgrn(ab3, s_a, lb, g_hgrn, first_layer=(layer == 0), tc=_pick(l, 512))
    ob, sb_new = _retention(ab3, s_b, rot[0], rot[1], tc=_pick(l, 256))
    q3, k3, v3 = (a.reshape(batch, l, W_CQ) for a in (q, kbf, vbf))
    out_scale = 1.0 - lam_init
    if cache is None:
        oc = _attn_prompt(q3, k3, v3, lam, g_diff, blk=256, out_scale=out_scale)
    else:
        oc = _attn_sample(q3, k3, v3, cache[0], cache[1], layer, lam, g_diff, out_scale=out_scale)
    x1, hm = _outproj(oa.reshape(t, W_A), ob.reshape(t, W_B), oc.reshape(t, W_C), w_out, x,
                      g_post_mix, g_pre_mlp, tm=tm)
    x2 = _mlp(hm, w_up, w_down, x1, g_post_mlp, tm=tm, tf=512)
    return (x2, sa_new, sb_new, k32.reshape(batch, l, H_C, 2 * DH_C), v32.reshape(batch, l, H_C, DV_C))


def kernel(x_prompt, x_sample, state_hgrn, state_ret, cache_k, cache_v, w_in, w_out, w_up, w_down, g_pre_mix, g_post_mix, g_pre_mlp, g_post_mlp, hgrn_lb_logits, g_hgrn_out, lambda_q1, lambda_k1, lambda_q2, lambda_k2, g_diff_out):
    depth = w_in.shape[0]
    b_p, seq_p, _ = x_prompt.shape
    b_s, n_new, _ = x_sample.shape
    past_len = cache_k.shape[2]

    rot_p = _rotary_tables(jnp.arange(seq_p, dtype=F32))
    rot_s = _rotary_tables(past_len + jnp.arange(n_new, dtype=F32))
    lb_soft = jax.nn.softmax(hgrn_lb_logits.astype(F32), axis=0)
    lower_bounds = jnp.cumsum(lb_soft, axis=0) - lb_soft[0]
    cache = (cache_k.reshape(depth, b_s, past_len, W_CQ), cache_v.reshape(depth, b_s, past_len, W_C))
    sa0 = jnp.zeros((b_p, H_A, DK_A, DV_A), F32)
    sb0 = jnp.zeros((b_p, H_B, DK_B, DV_B), F32)

    hp = x_prompt.reshape(b_p * seq_p, D_MODEL)
    hs = x_sample.reshape(b_s * n_new, D_MODEL)
    outs_p, outs_s = [], []
    for l in range(depth):
        lam_init = 0.8 - 0.6 * math.exp(-0.3 * l)
        lam = (jnp.exp(jnp.sum(lambda_q1[l] * lambda_k1[l])) - jnp.exp(jnp.sum(lambda_q2[l] * lambda_k2[l]))
               + lam_init).astype(F32).reshape(1, 1)
        vec = lambda a: a[l].reshape(1, -1)
        wts = (_prep_w_in(w_in[l]), w_out[l].astype(BF16), w_up[l].astype(BF16), w_down[l].astype(BF16),
               vec(g_pre_mix), vec(g_post_mix), vec(g_pre_mlp), vec(g_post_mlp), vec(g_hgrn_out),
               vec(g_diff_out))
        lb = lower_bounds[l].reshape(1, -1)
        hp, *rest_p = _layer(hp, b_p, sa0, sb0, rot_p, None, l, lb, lam, lam_init, wts)
        hs, *rest_s = _layer(hs, b_s, state_hgrn[l], state_ret[l], rot_s, cache, l, lb, lam, lam_init, wts)
        outs_p.append(rest_p)
        outs_s.append(rest_s)

    stack = lambda outs, i: jnp.stack([o[i] for o in outs])
    return (hp.reshape(b_p, seq_p, D_MODEL), hs.reshape(b_s, n_new, D_MODEL),
            stack(outs_p, 0), stack(outs_s, 0), stack(outs_p, 1), stack(outs_s, 1),
            stack(outs_p, 2), stack(outs_s, 2), stack(outs_p, 3), stack(outs_s, 3))
```

```python
import functools
import math

import jax
import jax.numpy as jnp
from jax import lax
from jax.experimental import pallas as pl
from jax.experimental.pallas import tpu as pltpu

F32 = jnp.float32
BF16 = jnp.bfloat16

D_MODEL = 2048
CHUNK = 64
EPS = 1e-6
H_A, DK_A, DV_A = 4, 128, 128
H_B, DK_B, DV_B = 4, 64, 128
RET_THETA_BASE = 10000.0
H_C, DH_C, DV_C = 4, 128, 256
D_FF = 4 * D_MODEL
W_A = H_A * DV_A
W_B = H_B * DV_B
W_C = H_C * DV_C
W_AB = 4 * H_A * DK_A + 2 * H_B * DK_B + 2 * H_B * DV_B
W_CQ = H_C * 2 * DH_C
IN_WIDTH = W_AB + 3 * W_CQ
HALF_B = DK_B // 2

VMEM_LIMIT_BYTES = 56 * 1024 * 1024
SUBCHUNK = 16
NEG_BIG = -1e30
LOG2_E = 1.4426950408889634
LANES = 128


def _cparams(n_axes):
    return pltpu.CompilerParams(dimension_semantics=("arbitrary",) * n_axes,
                                vmem_limit_bytes=VMEM_LIMIT_BYTES)


def _rms_scale(x):
    return lax.rsqrt(jnp.mean(x * x, axis=-1, keepdims=True) + EPS)


def _sigmoid(x):
    return 1.0 / (1.0 + jnp.exp(-x))


def _dot(a, b):
    return jnp.dot(a, b, preferred_element_type=F32)


def _dot_nt(a, b):
    return lax.dot_general(a, b, (((1,), (1,)), ((), ())), preferred_element_type=F32)


def _dot_tn(a, b):
    return lax.dot_general(a, b, (((0,), (0,)), ((), ())), preferred_element_type=F32)


def _inproj_kernel(*refs, n_ab, n_c, q_scale, aliased):
    x_ref, g_ref, w_ref = refs[:3]
    ab_ref, q_ref, kbf_ref, vbf_ref, krow_ref, vrow_ref, h_scr = refs[5 if aliased else 3:]
    j = pl.program_id(1)
    tn = w_ref.shape[1]
    heads_per_step = tn // DV_C

    @pl.when(j == 0)
    def _():
        x = x_ref[...]
        h_scr[...] = (x * _rms_scale(x) * g_ref[...]).astype(BF16)

    @pl.when(j < n_ab)
    def _():
        ab_ref[...] = _dot(h_scr[...], w_ref[...])

    @pl.when((j >= n_ab) & (j < n_ab + n_c))
    def _():
        q_ref[...] = (_dot(h_scr[...], w_ref[...]) * q_scale).astype(BF16)

    for group, (bf_ref, row_ref) in enumerate(((kbf_ref, krow_ref), (vbf_ref, vrow_ref))):
        for step in range(n_c):
            @pl.when(j == n_ab + (1 + group) * n_c + step)
            def _(bf_ref=bf_ref, row_ref=row_ref, step=step):
                acc = _dot(h_scr[...], w_ref[...])
                bf_ref[...] = acc.astype(BF16)
                for hh in range(heads_per_step):
                    row_ref[:, step * heads_per_step + hh, :] = acc[:, hh * DV_C:(hh + 1) * DV_C]


def _inproj(x, g, w, rows_prev, layer, depth, *, tm, tn):
    t = x.shape[0]
    n_ab, n_c = W_AB // tn, W_CQ // tn
    aliased = rows_prev is not None
    kern = functools.partial(_inproj_kernel, n_ab=n_ab, n_c=n_c, q_scale=DH_C ** -0.5 * LOG2_E,
                             aliased=aliased)

    def cmap(lo):
        return lambda i, j: (i, jnp.clip(j - lo, 0, n_c - 1))

    row_spec = pl.BlockSpec((None, tm, H_C, DV_C), lambda i, j: (layer, i, 0, 0))
    row_shape = jax.ShapeDtypeStruct((depth, t, H_C, DV_C), F32)
    in_specs = [pl.BlockSpec((tm, D_MODEL), lambda i, j: (i, 0)),
                pl.BlockSpec((1, D_MODEL), lambda i, j: (0, 0)),
                pl.BlockSpec((D_MODEL, tn), lambda i, j: (0, j))]
    args = [x, g, w]
    if aliased:
        in_specs += [pl.BlockSpec(memory_space=pl.ANY)] * 2
        args += list(rows_prev)
    return pl.pallas_call(
        kern,
        grid=(t // tm, IN_WIDTH // tn),
        in_specs=in_specs,
        out_specs=[pl.BlockSpec((tm, tn), lambda i, j: (i, jnp.minimum(j, n_ab - 1))),
                   pl.BlockSpec((tm, tn), cmap(n_ab)),
                   pl.BlockSpec((tm, tn), cmap(n_ab + n_c)),
                   pl.BlockSpec((tm, tn), cmap(n_ab + 2 * n_c)),
                   row_spec, row_spec],
        out_shape=[jax.ShapeDtypeStruct((t, W_AB), F32),
                   jax.ShapeDtypeStruct((t, W_CQ), BF16),
                   jax.ShapeDtypeStruct((t, W_CQ), BF16),
                   jax.ShapeDtypeStruct((t, W_CQ), BF16),
                   row_shape, row_shape],
        input_output_aliases={3: 4, 4: 5} if aliased else {},
        scratch_shapes=[pltpu.VMEM((tm, D_MODEL), BF16)],
        compiler_params=_cparams(2),
        name="inproj",
    )(*args)


def _hgrn_kernel(lb_ref, gout_ref, s0_ref, q_ref, f_ref, i_ref, g_ref, o_ref, sfin_ref, st_scr,
                 *, first_layer, n_chunks):
    t = pl.program_id(1)

    @pl.when(t == 0)
    def _():
        for h in range(H_A):
            st_scr[h] = s0_ref[h].T

    rows = lax.broadcasted_iota(jnp.int32, (CHUNK, CHUNK), 0)
    cols = lax.broadcasted_iota(jnp.int32, (CHUNK, CHUNK), 1)
    tri = (cols <= rows).astype(BF16)
    sub_rows = lax.broadcasted_iota(jnp.int32, (SUBCHUNK, 1), 0)
    gout = gout_ref[...]
    n_sub = CHUNK // SUBCHUNK

    def chunk_body(c, carry):
        r0 = pl.multiple_of(c * CHUNK, CHUNK)
        zq = q_ref[pl.ds(r0, CHUNK), :]
        zf = f_ref[pl.ds(r0, CHUNK), :]
        vi = i_ref[pl.ds(r0, CHUNK), :]
        zg = g_ref[pl.ds(r0, CHUNK), :]
        lb = lb_ref[...]

        e = jnp.exp(-jnp.abs(zf))
        r = 1.0 / (1.0 + e)
        pos = zf >= 0.0
        if first_layer:
            logf = jnp.minimum(zf, 0.0) - jnp.log(1.0 + e)
        else:
            logf = jnp.log(lb + (1.0 - lb) * jnp.where(pos, r, e * r))
        kk = (1.0 - lb) * jnp.where(pos, e * r, r)
        qs = zq * _sigmoid(zq) * (DK_A ** -0.5)
        gate = zg * _sigmoid(zg)

        g_hi = logf.astype(BF16)
        r1 = logf - g_hi.astype(F32)
        g_mid = r1.astype(BF16)
        g_lo = (r1 - g_mid.astype(F32)).astype(BF16)
        bcum = _dot(tri, g_hi) + _dot(tri, g_mid) + _dot(tri, g_lo)

        for h in range(H_A):
            sl = slice(h * DK_A, (h + 1) * DK_A)
            b = bcum[:, sl]
            q_h = qs[:, sl]
            k_h = kk[:, sl]
            v_h = vi[:, sl]
            v_bf = v_h.astype(BF16)
            st = st_scr[h]
            btot = b[CHUNK - 1:CHUNK, :]

            o_inter = _dot_nt((q_h * jnp.exp(b)).astype(BF16), st.astype(BF16))
            kdec = (k_h * jnp.exp(btot - b)).astype(BF16)
            st_scr[h] = st * jnp.exp(btot) + _dot_tn(v_bf, kdec)

            for i in range(n_sub):
                lo = i * SUBCHUNK
                b_i = b[lo:lo + SUBCHUNK, :]
                q_i = q_h[lo:lo + SUBCHUNK, :]
                o_i = o_inter[lo:lo + SUBCHUNK, :]
                if i > 0:
                    ref_row = b[lo - 1:lo, :]
                    qd = (q_i * jnp.exp(b_i - ref_row)).astype(BF16)
                    kd = (k_h[:lo, :] * jnp.exp(ref_row - b[:lo, :])).astype(BF16)
                    a = _dot_nt(qd, kd)
                    o_i = o_i + _dot(a.astype(BF16), v_bf[:lo, :])
                for s in range(SUBCHUNK):
                    sg = lo + s
                    d = jnp.exp(jnp.minimum(b_i - b[sg:sg + 1, :], 0.0))
                    w = jnp.sum(q_i * d * k_h[sg:sg + 1, :], axis=-1, keepdims=True)
                    w = jnp.where(sub_rows >= s, w, 0.0)
                    o_i = o_i + w * v_h[sg:sg + 1, :]
                y = o_i * _rms_scale(o_i) * gout * gate[lo:lo + SUBCHUNK, sl]
                o_ref[pl.ds(r0 + lo, SUBCHUNK), sl] = y.astype(BF16)
        return carry

    lax.fori_loop(0, n_chunks, chunk_body, 0)

    @pl.when(t == pl.num_programs(1) - 1)
    def _():
        for h in range(H_A):
            sfin_ref[h] = st_scr[h].T


def _hgrn(ab, s0, lb, gout, *, first_layer, tc):
    b, l, _ = ab.shape
    n_chunks = tc // CHUNK
    kern = functools.partial(_hgrn_kernel, first_layer=first_layer, n_chunks=n_chunks)
    blk = H_A * DK_A

    def col(k):
        return pl.BlockSpec((None, tc, blk), lambda bi, ti: (bi, ti, k))

    state_spec = pl.BlockSpec((None, H_A, DK_A, DV_A), lambda bi, ti: (bi, 0, 0, 0))
    return pl.pallas_call(
        kern,
        grid=(b, l // tc),
        in_specs=[pl.BlockSpec((1, blk), lambda bi, ti: (0, 0)),
                  pl.BlockSpec((1, DV_A), lambda bi, ti: (0, 0)),
                  state_spec, col(0), col(1), col(2), col(3)],
        out_specs=[pl.BlockSpec((None, tc, W_A), lambda bi, ti: (bi, ti, 0)), state_spec],
        out_shape=[jax.ShapeDtypeStruct((b, l, W_A), BF16),
                   jax.ShapeDtypeStruct((b, H_A, DK_A, DV_A), F32)],
        scratch_shapes=[pltpu.VMEM((H_A, DV_A, DK_A), F32)],
        compiler_params=_cparams(2),
        name="hgrn",
    )(lb, gout, s0, ab, ab, ab, ab)


_AB_RET_Q = (4 * H_A * DK_A) // (H_B * DK_B)
_AB_RET_V = (4 * H_A * DK_A + 2 * H_B * DK_B) // W_B
_LOG_GAMMA = tuple(math.log1p(-(2.0 ** (-5.0 - h))) for h in range(H_B))


def _ret_kernel(cos_ref, sin_ref, s0_ref, q_ref, k_ref, v_ref, g_ref, o_ref, sfin_ref, s_scr, *, tc):
    t = pl.program_id(1)
    wq = H_B * DK_B
    wh = H_B * HALF_B

    @pl.when(t == 0)
    def _():
        s_scr[...] = jnp.zeros_like(s_scr)
        for h in range(H_B):
            s_scr[h, h * HALF_B:(h + 1) * HALF_B, :] = s0_ref[h, :HALF_B, :]
            s_scr[h, wh + h * HALF_B:wh + (h + 1) * HALF_B, :] = s0_ref[h, HALF_B:, :]

    c = cos_ref[...]
    s = sin_ref[...]

    def rot(x):
        x1, x2 = x[:, :wh], x[:, wh:]
        return jnp.concatenate([x1 * c - x2 * s, x1 * s + x2 * c], axis=1)

    qr = rot(q_ref[...]).astype(BF16)
    kr = rot(k_ref[...]) * (DK_B ** -0.5)
    v = v_ref[...]
    gate_in = g_ref[...]

    lane_head = (lax.broadcasted_iota(jnp.int32, (1, wq), 1) % wh) // HALF_B
    rel = (lax.broadcasted_iota(jnp.int32, (tc, tc), 0)
           - lax.broadcasted_iota(jnp.int32, (tc, tc), 1))
    relf = jnp.maximum(rel, 0).astype(F32)
    tpos = lax.broadcasted_iota(jnp.int32, (tc, 1), 0).astype(F32)

    for h in range(H_B):
        lg = _LOG_GAMMA[h]
        sl = slice(h * DV_B, (h + 1) * DV_B)
        kh = jnp.where(lane_head == h, kr, 0.0)
        v_bf = v[:, sl].astype(BF16)
        dmat = jnp.where(rel >= 0, jnp.exp(relf * lg), 0.0)
        sc = _dot_nt(qr, kh.astype(BF16)) * dmat
        st = s_scr[h]
        o = _dot(sc.astype(BF16), v_bf) + _dot(qr, st.astype(BF16)) * jnp.exp((tpos + 1.0) * lg)
        ktail = (kh * jnp.exp((tc - 1.0 - tpos) * lg)).astype(BF16)
        s_scr[h] = math.exp(tc * lg) * st + _dot_tn(ktail, v_bf)
        zg = gate_in[:, sl]
        o_ref[:, sl] = (o * _rms_scale(o) * (zg * _sigmoid(zg))).astype(BF16)

    @pl.when(t == pl.num_programs(1) - 1)
    def _():
        for h in range(H_B):
            sfin_ref[h, :HALF_B, :] = s_scr[h, h * HALF_B:(h + 1) * HALF_B, :]
            sfin_ref[h, HALF_B:, :] = s_scr[h, wh + h * HALF_B:wh + (h + 1) * HALF_B, :]


def _retention(ab, s0, cos, sin, *, tc):
    b, l, _ = ab.shape
    wq = H_B * DK_B
    wh = H_B * HALF_B
    kern = functools.partial(_ret_kernel, tc=tc)
    state_spec = pl.BlockSpec((None, H_B, DK_B, DV_B), lambda bi, ti: (bi, 0, 0, 0))
    return pl.pallas_call(
        kern,
        grid=(b, l // tc),
        in_specs=[pl.BlockSpec((tc, wh), lambda bi, ti: (ti, 0)),
                  pl.BlockSpec((tc, wh), lambda bi, ti: (ti, 0)),
                  state_spec,
                  pl.BlockSpec((None, tc, wq), lambda bi, ti: (bi, ti, _AB_RET_Q)),
                  pl.BlockSpec((None, tc, wq), lambda bi, ti: (bi, ti, _AB_RET_Q + 1)),
                  pl.BlockSpec((None, tc, W_B), lambda bi, ti: (bi, ti, _AB_RET_V)),
                  pl.BlockSpec((None, tc, W_B), lambda bi, ti: (bi, ti, _AB_RET_V + 1))],
        out_specs=[pl.BlockSpec((None, tc, W_B), lambda bi, ti: (bi, ti, 0)), state_spec],
        out_shape=[jax.ShapeDtypeStruct((b, l, W_B), BF16),
                   jax.ShapeDtypeStruct((b, H_B, DK_B, DV_B), F32)],
        scratch_shapes=[pltpu.VMEM((H_B, wq, DV_B), F32)],
        compiler_params=_cparams(2),
        name="retention",
    )(cos, sin, s0, ab, ab, ab, ab)


def _diff_out(o1, l1, o2, l2, lam, g, out_scale):
    o = o1 / l1 - lam * (o2 / l2)
    return (o * _rms_scale(o) * g * out_scale).astype(BF16)


def _attn_prompt_kernel(lam_ref, q_ref, k_ref, v_ref, g_ref, o_ref, acc_scr, sa_scr, sb_scr, m_scr, l_scr,
                        *, blk, out_scale):
    qi = pl.program_id(2)
    q = q_ref[...]
    qh = (q[:, :DH_C], q[:, DH_C:])
    rq = lax.broadcasted_iota(jnp.int32, (blk, blk), 0) // CHUNK
    ck = lax.broadcasted_iota(jnp.int32, (blk, blk), 1) // CHUNK
    visible = ck <= rq
    acc_scr[...] = jnp.zeros_like(acc_scr)
    m_scr[...] = jnp.full(m_scr.shape, NEG_BIG, F32)
    l_scr[...] = jnp.zeros_like(l_scr)

    def produce(j, buf):
        off = pl.multiple_of(j * blk, blk)
        kb = k_ref[pl.ds(off, blk), :]
        for half in range(2):
            buf[half] = _dot_nt(qh[half], kb[:, half * DH_C:(half + 1) * DH_C])

    def consume(j, buf, masked):
        off = pl.multiple_of(j * blk, blk)
        vb = v_ref[pl.ds(off, blk), :]
        for half in range(2):
            m_old = m_scr[half]
            s = buf[half]
            if masked:
                s = jnp.where(visible, s, -jnp.inf)
            m_new = jnp.maximum(m_old, jnp.max(s, axis=-1, keepdims=True))
            alpha = jnp.exp2(m_old - m_new)
            p = jnp.exp2(s - pltpu.repeat(m_new, blk // LANES, axis=1))
            l_scr[half] = alpha * l_scr[half] + jnp.sum(p, axis=-1, keepdims=True)
            acc_scr[half] = (pltpu.repeat(alpha, DV_C // LANES, axis=1) * acc_scr[half]
                             + _dot(p.astype(BF16), vb))
            m_scr[half] = m_new

    produce(0, sa_scr)

    def pair(t, carry):
        produce(2 * t + 1, sb_scr)
        consume(2 * t, sa_scr, False)
        produce(2 * t + 2, sa_scr)
        consume(2 * t + 1, sb_scr, False)
        return carry

    lax.fori_loop(0, qi // 2, pair, 0)

    @pl.when(qi % 2 == 0)
    def _():
        consume(qi, sa_scr, True)

    @pl.when(qi % 2 == 1)
    def _():
        produce(qi, sb_scr)
        consume(qi - 1, sa_scr, False)
        consume(qi, sb_scr, True)

    o_ref[...] = _diff_out(acc_scr[0], l_scr[0][:, :1], acc_scr[1], l_scr[1][:, :1],
                           lam_ref[0, 0], g_ref[...], out_scale)


def _attn_prompt(q, k, v, lam, g, *, blk, out_scale):
    b, l, _ = q.shape
    kern = functools.partial(_attn_prompt_kernel, blk=blk, out_scale=out_scale)
    kv_spec = pl.BlockSpec((None, l, DV_C), lambda bi, hi, qi: (bi, 0, hi))
    return pl.pallas_call(
        kern,
        grid=(b, H_C, l // blk),
        in_specs=[pl.BlockSpec(memory_space=pltpu.SMEM),
                  pl.BlockSpec((None, blk, DV_C), lambda bi, hi, qi: (bi, qi, hi)),
                  kv_spec, kv_spec,
                  pl.BlockSpec((1, DV_C), lambda bi, hi, qi: (0, 0))],
        out_specs=pl.BlockSpec((None, blk, DV_C), lambda bi, hi, qi: (bi, qi, hi)),
        out_shape=jax.ShapeDtypeStruct((b, l, W_C), BF16),
        scratch_shapes=[pltpu.VMEM((2, blk, DV_C), F32),
                        pltpu.VMEM((2, blk, blk), F32),
                        pltpu.VMEM((2, blk, blk), F32),
                        pltpu.VMEM((2, blk, LANES), F32),
                        pltpu.VMEM((2, blk, LANES), F32)],
        compiler_params=_cparams(3),
        name="attn_prompt",
    )(lam, q, k, v, g)


def _attn_sample_kernel(lam_ref, q_ref, kn_ref, vn_ref, kp_ref, vp_ref, g_ref, o_ref, *, out_scale):
    q = q_ref[...]
    kn = kn_ref[...]
    vn = vn_ref[...]
    kp = kp_ref[...].astype(BF16)
    vp = vp_ref[...].astype(BF16)
    outs = []
    for half in range(2):
        hs = slice(half * DH_C, (half + 1) * DH_C)
        s_p = _dot_nt(q[:, hs], kp[:, hs])
        s_n = _dot_nt(q[:, hs], kn[:, hs])
        m = jnp.maximum(jnp.max(s_p, axis=-1, keepdims=True), jnp.max(s_n, axis=-1, keepdims=True))
        p_p = jnp.exp2(s_p - m)
        p_n = jnp.exp2(s_n - m)
        l = jnp.sum(p_p, axis=-1, keepdims=True) + jnp.sum(p_n, axis=-1, keepdims=True)
        outs += [_dot(p_p.astype(BF16), vp) + _dot(p_n.astype(BF16), vn), l]
    o_ref[...] = _diff_out(outs[0], outs[1], outs[2], outs[3], lam_ref[0, 0], g_ref[...], out_scale)


def _attn_sample(q, kn, vn, cache_k, cache_v, layer, lam, g, *, out_scale):
    b, n, _ = q.shape
    past = cache_k.shape[2]
    kern = functools.partial(_attn_sample_kernel, out_scale=out_scale)
    new_spec = pl.BlockSpec((None, n, DV_C), lambda bi, hi: (bi, 0, hi))
    past_spec = pl.BlockSpec((None, None, past, DV_C), lambda bi, hi: (layer, bi, 0, hi))
    return pl.pallas_call(
        kern,
        grid=(b, H_C),
        in_specs=[pl.BlockSpec(memory_space=pltpu.SMEM),
                  new_spec, new_spec, new_spec, past_spec, past_spec,
                  pl.BlockSpec((1, DV_C), lambda bi, hi: (0, 0))],
        out_specs=new_spec,
        out_shape=jax.ShapeDtypeStruct((b, n, W_C), BF16),
        compiler_params=_cparams(2),
        name="attn_sample",
    )(lam, q, kn, vn, cache_k, cache_v, g)


def _outproj_kernel(oa_ref, ob_ref, oc_ref, wa_ref, wb_ref, wc_ref, x_ref, gpost_ref, gpre_ref,
                    x1_ref, hm_ref):
    y = _dot(oa_ref[...], wa_ref[...]) + _dot(ob_ref[...], wb_ref[...]) + _dot(oc_ref[...], wc_ref[...])
    x1 = x_ref[...] + y * _rms_scale(y) * gpost_ref[...]
    x1_ref[...] = x1
    hm_ref[...] = (x1 * _rms_scale(x1) * gpre_ref[...]).astype(BF16)


def _outproj(oa, ob, oc, w_out, x, gpost, gpre, *, tm):
    t = x.shape[0]
    row = lambda w: pl.BlockSpec((tm, w), lambda i: (i, 0))
    vec = pl.BlockSpec((1, D_MODEL), lambda i: (0, 0))
    return pl.pallas_call(
        _outproj_kernel,
        grid=(t // tm,),
        in_specs=[row(W_A), row(W_B), row(W_C),
                  pl.BlockSpec((W_A, D_MODEL), lambda i: (0, 0)),
                  pl.BlockSpec((W_B, D_MODEL), lambda i: (1, 0)),
                  pl.BlockSpec((W_C, D_MODEL), lambda i: (1, 0)),
                  row(D_MODEL), vec, vec],
        out_specs=[row(D_MODEL), row(D_MODEL)],
        out_shape=[jax.ShapeDtypeStruct((t, D_MODEL), F32),
                   jax.ShapeDtypeStruct((t, D_MODEL), BF16)],
        compiler_params=_cparams(1),
        name="outproj",
    )(oa, ob, oc, w_out, w_out, w_out, x, gpost, gpre)


def _mlp_kernel(hm_ref, wu_ref, wd_ref, x1_ref, g_ref, o_ref, acc_scr):
    f = pl.program_id(1)

    @pl.when(f == 0)
    def _():
        acc_scr[...] = jnp.zeros_like(acc_scr)

    u = jnp.maximum(_dot(hm_ref[...], wu_ref[...]), 0.0)
    acc_scr[...] += _dot((u * u).astype(BF16), wd_ref[...])

    @pl.when(f == pl.num_programs(1) - 1)
    def _():
        y = acc_scr[...]
        o_ref[...] = x1_ref[...] + y * _rms_scale(y) * g_ref[...]


def _mlp(hm, w_up, w_down, x1, g, *, tm, tf):
    t = hm.shape[0]
    row = lambda dt: pl.BlockSpec((tm, D_MODEL), lambda i, f: (i, 0))
    return pl.pallas_call(
        _mlp_kernel,
        grid=(t // tm, D_FF // tf),
        in_specs=[row(BF16),
                  pl.BlockSpec((D_MODEL, tf), lambda i, f: (0, f)),
                  pl.BlockSpec((tf, D_MODEL), lambda i, f: (f, 0)),
                  row(F32),
                  pl.BlockSpec((1, D_MODEL), lambda i, f: (0, 0))],
        out_specs=row(F32),
        out_shape=jax.ShapeDtypeStruct((t, D_MODEL), F32),
        scratch_shapes=[pltpu.VMEM((tm, D_MODEL), F32)],
        compiler_params=_cparams(2),
        name="mlp",
    )(hm, w_up, w_down, x1, g)


def _rotary_tables(pos):
    inv = 1.0 / (RET_THETA_BASE ** jnp.linspace(0.0, 1.0, HALF_B, dtype=F32))
    ang = pos[:, None] * inv[None]
    return jnp.tile(jnp.cos(ang), (1, H_B)), jnp.tile(jnp.sin(ang), (1, H_B))


def _prep_w_in(w):
    def regroup(cols):
        return cols.reshape(D_MODEL, H_B, 2, HALF_B).transpose(0, 2, 1, 3).reshape(D_MODEL, H_B * DK_B)

    a_end = 4 * H_A * DK_A
    wq = H_B * DK_B
    parts = [w[:, :a_end], regroup(w[:, a_end:a_end + wq]),
             regroup(w[:, a_end + wq:a_end + 2 * wq]), w[:, a_end + 2 * wq:]]
    return jnp.concatenate(parts, axis=1).astype(BF16)


def _pick(n, pref):
    return pref if n % pref == 0 else n


def _layer(x, batch, s_a, s_b, rot, cache, rows_prev, layer, depth, lb, lam, lam_init, wts):
    w_in, w_out, w_up, w_down, g_pre_mix, g_post_mix, g_pre_mlp, g_post_mlp, g_hgrn, g_diff = wts
    t = x.shape[0]
    l = t // batch
    tm = _pick(t, 512)
    ab, q, kbf, vbf, k_rows, v_rows = _inproj(x, g_pre_mix, w_in, rows_prev, layer, depth, tm=tm, tn=512)
    ab3 = ab.reshape(batch, l, W_AB)
    oa, sa_new = _hgrn(ab3, s_a, lb, g_hgrn, first_layer=(layer == 0), tc=_pick(l, 512))
    ob, sb_new = _retention(ab3, s_b, rot[0], rot[1], tc=_pick(l, 256))
    q3, k3, v3 = (a.reshape(batch, l, W_CQ) for a in (q, kbf, vbf))
    out_scale = 1.0 - lam_init
    if cache is None:
        oc = _attn_prompt(q3, k3, v3, lam, g_diff, blk=512, out_scale=out_scale)
    else:
        oc = _attn_sample(q3, k3, v3, cache[0], cache[1], layer, lam, g_diff, out_scale=out_scale)
    x1, hm = _outproj(oa.reshape(t, W_A), ob.reshape(t, W_B), oc.reshape(t, W_C), w_out, x,
                      g_post_mix, g_pre_mlp, tm=tm)
    x2 = _mlp(hm, w_up, w_down, x1, g_post_mlp, tm=tm, tf=1024)
    return x2, sa_new, sb_new, (k_rows, v_rows)


def kernel(x_prompt, x_sample, state_hgrn, state_ret, cache_k, cache_v, w_in, w_out, w_up, w_down, g_pre_mix, g_post_mix, g_pre_mlp, g_post_mlp, hgrn_lb_logits, g_hgrn_out, lambda_q1, lambda_k1, lambda_q2, lambda_k2, g_diff_out):
    depth = w_in.shape[0]
    b_p, seq_p, _ = x_prompt.shape
    b_s, n_new, _ = x_sample.shape
    past_len = cache_k.shape[2]

    rot_p = _rotary_tables(jnp.arange(seq_p, dtype=F32))
    rot_s = _rotary_tables(past_len + jnp.arange(n_new, dtype=F32))
    lb_soft = jax.nn.softmax(hgrn_lb_logits.astype(F32), axis=0)
    lower_bounds = jnp.cumsum(lb_soft, axis=0) - lb_soft[0]
    cache = (cache_k.reshape(depth, b_s, past_len, W_CQ), cache_v.reshape(depth, b_s, past_len, W_C))
    sa0 = jnp.zeros((b_p, H_A, DK_A, DV_A), F32)
    sb0 = jnp.zeros((b_p, H_B, DK_B, DV_B), F32)

    hp = x_prompt.reshape(b_p * seq_p, D_MODEL)
    hs = x_sample.reshape(b_s * n_new, D_MODEL)
    states_p, states_s = [], []
    rows_p = rows_s = None
    for l in range(depth):
        lam_init = 0.8 - 0.6 * math.exp(-0.3 * l)
        lam = (jnp.exp(jnp.sum(lambda_q1[l] * lambda_k1[l])) - jnp.exp(jnp.sum(lambda_q2[l] * lambda_k2[l]))
               + lam_init).astype(F32).reshape(1, 1)
        vec = lambda a: a[l].reshape(1, -1)
        wts = (_prep_w_in(w_in[l]), w_out[l].astype(BF16), w_up[l].astype(BF16), w_down[l].astype(BF16),
               vec(g_pre_mix), vec(g_post_mix), vec(g_pre_mlp), vec(g_post_mlp), vec(g_hgrn_out),
               vec(g_diff_out))
        lb = lower_bounds[l].reshape(1, -1)
        hp, sa_p, sb_p, rows_p = _layer(hp, b_p, sa0, sb0, rot_p, None, rows_p, l, depth, lb, lam,
                                        lam_init, wts)
        hs, sa_s, sb_s, rows_s = _layer(hs, b_s, state_hgrn[l], state_ret[l], rot_s, cache, rows_s, l, depth,
                                        lb, lam, lam_init, wts)
        states_p.append((sa_p, sb_p))
        states_s.append((sa_s, sb_s))

    stack = lambda outs, i: jnp.stack([o[i] for o in outs])
    rows = lambda a, b, n: a.reshape(depth, b, n, H_C, DV_C)
    return (hp.reshape(b_p, seq_p, D_MODEL), hs.reshape(b_s, n_new, D_MODEL),
            stack(states_p, 0), stack(states_s, 0), stack(states_p, 1), stack(states_s, 1),
            rows(rows_p[0], b_p, seq_p), rows(rows_s[0], b_s, n_new),
            rows(rows_p[1], b_p, seq_p), rows(rows_s[1], b_s, n_new))
```

```python
import functools
import math

import jax
import jax.numpy as jnp
from jax import lax
from jax.experimental import pallas as pl
from jax.experimental.pallas import tpu as pltpu

F32 = jnp.float32
BF16 = jnp.bfloat16

D_MODEL = 2048
CHUNK = 64
EPS = 1e-6
H_A, DK_A, DV_A = 4, 128, 128
H_B, DK_B, DV_B = 4, 64, 128
RET_THETA_BASE = 10000.0
H_C, DH_C, DV_C = 4, 128, 256
D_FF = 4 * D_MODEL
W_A = H_A * DV_A
W_B = H_B * DV_B
W_C = H_C * DV_C
W_AB = 4 * H_A * DK_A + 2 * H_B * DK_B + 2 * H_B * DV_B
W_CQ = H_C * 2 * DH_C
IN_WIDTH = W_AB + 3 * W_CQ
HALF_B = DK_B // 2

VMEM_LIMIT_BYTES = 56 * 1024 * 1024
SUBCHUNK = 16
NEG_BIG = -1e30
LOG2_E = 1.4426950408889634
LANES = 128


def _cparams(n_axes):
    return pltpu.CompilerParams(dimension_semantics=("arbitrary",) * n_axes,
                                vmem_limit_bytes=VMEM_LIMIT_BYTES)


def _rms_scale(x):
    return lax.rsqrt(jnp.mean(x * x, axis=-1, keepdims=True) + EPS)


def _sigmoid(x):
    return 1.0 / (1.0 + jnp.exp(-x))


def _dot(a, b):
    return jnp.dot(a, b, preferred_element_type=F32)


def _dot_nt(a, b):
    return lax.dot_general(a, b, (((1,), (1,)), ((), ())), preferred_element_type=F32)


def _dot_tn(a, b):
    return lax.dot_general(a, b, (((0,), (0,)), ((), ())), preferred_element_type=F32)


def _inproj_kernel(*refs, q_scale, aliased, tn):
    x_ref, g_ref, w_ref = refs[:3]
    ab_ref, q_ref, kbf_ref, vbf_ref, krow_ref, vrow_ref, h_scr = refs[5 if aliased else 3:]
    x = x_ref[...]
    h_scr[...] = (x * _rms_scale(x) * g_ref[...]).astype(BF16)
    heads_per_group = tn // DV_C

    def cols(lo):
        return _dot(h_scr[...], w_ref[:, lo:lo + tn])

    for c in range(W_AB // tn):
        ab_ref[:, c * tn:(c + 1) * tn] = cols(c * tn)
    for c in range(W_CQ // tn):
        q_ref[:, c * tn:(c + 1) * tn] = (cols(W_AB + c * tn) * q_scale).astype(BF16)
    for group, (bf_ref, row_ref) in enumerate(((kbf_ref, krow_ref), (vbf_ref, vrow_ref))):
        for c in range(W_CQ // tn):
            acc = cols(W_AB + (1 + group) * W_CQ + c * tn)
            bf_ref[:, c * tn:(c + 1) * tn] = acc.astype(BF16)
            for hh in range(heads_per_group):
                row_ref[:, c * heads_per_group + hh, :] = acc[:, hh * DV_C:(hh + 1) * DV_C]


def _inproj(x, g, w, rows_prev, layer, depth, *, tm, tn):
    t = x.shape[0]
    aliased = rows_prev is not None
    kern = functools.partial(_inproj_kernel, q_scale=DH_C ** -0.5 * LOG2_E, aliased=aliased, tn=tn)
    row_spec = pl.BlockSpec((None, tm, H_C, DV_C), lambda i: (layer, i, 0, 0))
    row_shape = jax.ShapeDtypeStruct((depth, t, H_C, DV_C), F32)
    tok = lambda width: pl.BlockSpec((tm, width), lambda i: (i, 0))
    in_specs = [tok(D_MODEL),
                pl.BlockSpec((1, D_MODEL), lambda i: (0, 0)),
                pl.BlockSpec((D_MODEL, IN_WIDTH), lambda i: (0, 0), pipeline_mode=pl.Buffered(1))]
    args = [x, g, w]
    if aliased:
        in_specs += [pl.BlockSpec(memory_space=pl.ANY)] * 2
        args += list(rows_prev)
    return pl.pallas_call(
        kern,
        grid=(t // tm,),
        in_specs=in_specs,
        out_specs=[tok(W_AB), tok(W_CQ), tok(W_CQ), tok(W_CQ), row_spec, row_spec],
        out_shape=[jax.ShapeDtypeStruct((t, W_AB), F32),
                   jax.ShapeDtypeStruct((t, W_CQ), BF16),
                   jax.ShapeDtypeStruct((t, W_CQ), BF16),
                   jax.ShapeDtypeStruct((t, W_CQ), BF16),
                   row_shape, row_shape],
        input_output_aliases={3: 4, 4: 5} if aliased else {},
        scratch_shapes=[pltpu.VMEM((tm, D_MODEL), BF16)],
        compiler_params=_cparams(1),
        name="inproj",
    )(*args)


def _hgrn_kernel(lb_ref, gout_ref, s0_ref, q_ref, f_ref, i_ref, g_ref, o_ref, sfin_ref, st_scr,
                 *, first_layer, n_chunks):
    t = pl.program_id(1)

    @pl.when(t == 0)
    def _():
        for h in range(H_A):
            st_scr[h] = s0_ref[h].T

    rows = lax.broadcasted_iota(jnp.int32, (CHUNK, CHUNK), 0)
    cols = lax.broadcasted_iota(jnp.int32, (CHUNK, CHUNK), 1)
    tri = (cols <= rows).astype(BF16)
    sub_rows = lax.broadcasted_iota(jnp.int32, (SUBCHUNK, 1), 0)
    gout = gout_ref[...]
    n_sub = CHUNK // SUBCHUNK

    def chunk_body(c, carry):
        r0 = pl.multiple_of(c * CHUNK, CHUNK)
        zq = q_ref[pl.ds(r0, CHUNK), :]
        zf = f_ref[pl.ds(r0, CHUNK), :]
        vi = i_ref[pl.ds(r0, CHUNK), :]
        zg = g_ref[pl.ds(r0, CHUNK), :]
        lb = lb_ref[...]

        e = jnp.exp(-jnp.abs(zf))
        r = 1.0 / (1.0 + e)
        pos = zf >= 0.0
        if first_layer:
            logf = jnp.minimum(zf, 0.0) - jnp.log(1.0 + e)
        else:
            logf = jnp.log(lb + (1.0 - lb) * jnp.where(pos, r, e * r))
        kk = (1.0 - lb) * jnp.where(pos, e * r, r)
        qs = zq * _sigmoid(zq) * (DK_A ** -0.5)
        gate = zg * _sigmoid(zg)

        g_hi = logf.astype(BF16)
        r1 = logf - g_hi.astype(F32)
        g_mid = r1.astype(BF16)
        g_lo = (r1 - g_mid.astype(F32)).astype(BF16)
        bcum = _dot(tri, g_hi) + _dot(tri, g_mid) + _dot(tri, g_lo)

        for h in range(H_A):
            sl = slice(h * DK_A, (h + 1) * DK_A)
            b = bcum[:, sl]
            q_h = qs[:, sl]
            k_h = kk[:, sl]
            v_h = vi[:, sl]
            v_bf = v_h.astype(BF16)
            st = st_scr[h]
            btot = b[CHUNK - 1:CHUNK, :]

            o_inter = _dot_nt((q_h * jnp.exp(b)).astype(BF16), st.astype(BF16))
            kdec = (k_h * jnp.exp(btot - b)).astype(BF16)
            st_scr[h] = st * jnp.exp(btot) + _dot_tn(v_bf, kdec)

            for i in range(n_sub):
                lo = i * SUBCHUNK
                b_i = b[lo:lo + SUBCHUNK, :]
                q_i = q_h[lo:lo + SUBCHUNK, :]
                o_i = o_inter[lo:lo + SUBCHUNK, :]
                if i > 0:
                    ref_row = b[lo - 1:lo, :]
                    qd = (q_i * jnp.exp(b_i - ref_row)).astype(BF16)
                    kd = (k_h[:lo, :] * jnp.exp(ref_row - b[:lo, :])).astype(BF16)
                    a = _dot_nt(qd, kd)
                    o_i = o_i + _dot(a.astype(BF16), v_bf[:lo, :])
                for s in range(SUBCHUNK):
                    sg = lo + s
                    d = jnp.exp(jnp.minimum(b_i - b[sg:sg + 1, :], 0.0))
                    w = jnp.sum(q_i * d * k_h[sg:sg + 1, :], axis=-1, keepdims=True)
                    w = jnp.where(sub_rows >= s, w, 0.0)
                    o_i = o_i + w * v_h[sg:sg + 1, :]
                y = o_i * _rms_scale(o_i) * gout * gate[lo:lo + SUBCHUNK, sl]
                o_ref[pl.ds(r0 + lo, SUBCHUNK), sl] = y.astype(BF16)
        return carry

    lax.fori_loop(0, n_chunks, chunk_body, 0)

    @pl.when(t == pl.num_programs(1) - 1)
    def _():
        for h in range(H_A):
            sfin_ref[h] = st_scr[h].T


def _hgrn(ab, s0, lb, gout, *, first_layer, tc):
    b, l, _ = ab.shape
    n_chunks = tc // CHUNK
    kern = functools.partial(_hgrn_kernel, first_layer=first_layer, n_chunks=n_chunks)
    blk = H_A * DK_A

    def col(k):
        return pl.BlockSpec((None, tc, blk), lambda bi, ti: (bi, ti, k))

    state_spec = pl.BlockSpec((None, H_A, DK_A, DV_A), lambda bi, ti: (bi, 0, 0, 0))
    return pl.pallas_call(
        kern,
        grid=(b, l // tc),
        in_specs=[pl.BlockSpec((1, blk), lambda bi, ti: (0, 0)),
                  pl.BlockSpec((1, DV_A), lambda bi, ti: (0, 0)),
                  state_spec, col(0), col(1), col(2), col(3)],
        out_specs=[pl.BlockSpec((None, tc, W_A), lambda bi, ti: (bi, ti, 0)), state_spec],
        out_shape=[jax.ShapeDtypeStruct((b, l, W_A), BF16),
                   jax.ShapeDtypeStruct((b, H_A, DK_A, DV_A), F32)],
        scratch_shapes=[pltpu.VMEM((H_A, DV_A, DK_A), F32)],
        compiler_params=_cparams(2),
        name="hgrn",
    )(lb, gout, s0, ab, ab, ab, ab)


_AB_RET_Q = (4 * H_A * DK_A) // (H_B * DK_B)
_AB_RET_V = (4 * H_A * DK_A + 2 * H_B * DK_B) // W_B
_LOG_GAMMA = tuple(math.log1p(-(2.0 ** (-5.0 - h))) for h in range(H_B))


def _ret_kernel(cos_ref, sin_ref, s0_ref, q_ref, k_ref, v_ref, g_ref, o_ref, sfin_ref, s_scr, *, tc):
    t = pl.program_id(1)
    wq = H_B * DK_B
    wh = H_B * HALF_B

    @pl.when(t == 0)
    def _():
        s_scr[...] = jnp.zeros_like(s_scr)
        for h in range(H_B):
            s_scr[h, h * HALF_B:(h + 1) * HALF_B, :] = s0_ref[h, :HALF_B, :]
            s_scr[h, wh + h * HALF_B:wh + (h + 1) * HALF_B, :] = s0_ref[h, HALF_B:, :]

    c = cos_ref[...]
    s = sin_ref[...]

    def rot(x):
        x1, x2 = x[:, :wh], x[:, wh:]
        return jnp.concatenate([x1 * c - x2 * s, x1 * s + x2 * c], axis=1)

    qr = rot(q_ref[...]).astype(BF16)
    kr = rot(k_ref[...]) * (DK_B ** -0.5)
    v = v_ref[...]
    gate_in = g_ref[...]

    lane_head = (lax.broadcasted_iota(jnp.int32, (1, wq), 1) % wh) // HALF_B
    rel = (lax.broadcasted_iota(jnp.int32, (tc, tc), 0)
           - lax.broadcasted_iota(jnp.int32, (tc, tc), 1))
    relf = jnp.maximum(rel, 0).astype(F32)
    tpos = lax.broadcasted_iota(jnp.int32, (tc, 1), 0).astype(F32)

    for h in range(H_B):
        lg = _LOG_GAMMA[h]
        sl = slice(h * DV_B, (h + 1) * DV_B)
        kh = jnp.where(lane_head == h, kr, 0.0)
        v_bf = v[:, sl].astype(BF16)
        dmat = jnp.where(rel >= 0, jnp.exp(relf * lg), 0.0)
        sc = _dot_nt(qr, kh.astype(BF16)) * dmat
        st = s_scr[h]
        o = _dot(sc.astype(BF16), v_bf) + _dot(qr, st.astype(BF16)) * jnp.exp((tpos + 1.0) * lg)
        ktail = (kh * jnp.exp((tc - 1.0 - tpos) * lg)).astype(BF16)
        s_scr[h] = math.exp(tc * lg) * st + _dot_tn(ktail, v_bf)
        zg = gate_in[:, sl]
        o_ref[:, sl] = (o * _rms_scale(o) * (zg * _sigmoid(zg))).astype(BF16)

    @pl.when(t == pl.num_programs(1) - 1)
    def _():
        for h in range(H_B):
            sfin_ref[h, :HALF_B, :] = s_scr[h, h * HALF_B:(h + 1) * HALF_B, :]
            sfin_ref[h, HALF_B:, :] = s_scr[h, wh + h * HALF_B:wh + (h + 1) * HALF_B, :]


def _retention(ab, s0, cos, sin, *, tc):
    b, l, _ = ab.shape
    wq = H_B * DK_B
    wh = H_B * HALF_B
    kern = functools.partial(_ret_kernel, tc=tc)
    state_spec = pl.BlockSpec((None, H_B, DK_B, DV_B), lambda bi, ti: (bi, 0, 0, 0))
    return pl.pallas_call(
        kern,
        grid=(b, l // tc),
        in_specs=[pl.BlockSpec((tc, wh), lambda bi, ti: (ti, 0)),
                  pl.BlockSpec((tc, wh), lambda bi, ti: (ti, 0)),
                  state_spec,
                  pl.BlockSpec((None, tc, wq), lambda bi, ti: (bi, ti, _AB_RET_Q)),
                  pl.BlockSpec((None, tc, wq), lambda bi, ti: (bi, ti, _AB_RET_Q + 1)),
                  pl.BlockSpec((None, tc, W_B), lambda bi, ti: (bi, ti, _AB_RET_V)),
                  pl.BlockSpec((None, tc, W_B), lambda bi, ti: (bi, ti, _AB_RET_V + 1))],
        out_specs=[pl.BlockSpec((None, tc, W_B), lambda bi, ti: (bi, ti, 0)), state_spec],
        out_shape=[jax.ShapeDtypeStruct((b, l, W_B), BF16),
                   jax.ShapeDtypeStruct((b, H_B, DK_B, DV_B), F32)],
        scratch_shapes=[pltpu.VMEM((H_B, wq, DV_B), F32)],
        compiler_params=_cparams(2),
        name="retention",
    )(cos, sin, s0, ab, ab, ab, ab)


def _diff_out(o1, l1, o2, l2, lam, g, out_scale):
    o = o1 / l1 - lam * (o2 / l2)
    return (o * _rms_scale(o) * g * out_scale).astype(BF16)


def _attn_prompt_kernel(lam_ref, q_ref, k_ref, v_ref, g_ref, o_ref, acc_scr, sa_scr, sb_scr, m_scr, l_scr,
                        *, blk, out_scale):
    qi = pl.program_id(2)
    q = q_ref[...]
    qh = (q[:, :DH_C], q[:, DH_C:])
    rq = lax.broadcasted_iota(jnp.int32, (blk, blk), 0) // CHUNK
    ck = lax.broadcasted_iota(jnp.int32, (blk, blk), 1) // CHUNK
    visible = ck <= rq
    acc_scr[...] = jnp.zeros_like(acc_scr)
    m_scr[...] = jnp.full(m_scr.shape, NEG_BIG, F32)
    l_scr[...] = jnp.zeros_like(l_scr)

    def produce(j, buf):
        off = pl.multiple_of(j * blk, blk)
        kb = k_ref[pl.ds(off, blk), :]
        for half in range(2):
            buf[half] = _dot_nt(qh[half], kb[:, half * DH_C:(half + 1) * DH_C])

    def consume(j, buf, masked):
        off = pl.multiple_of(j * blk, blk)
        vb = v_ref[pl.ds(off, blk), :]
        for half in range(2):
            m_old = m_scr[half]
            s = buf[half]
            if masked:
                s = jnp.where(visible, s, -jnp.inf)
            m_new = jnp.maximum(m_old, jnp.max(s, axis=-1, keepdims=True))
            alpha = jnp.exp2(m_old - m_new)
            p = jnp.exp2(s - jnp.concatenate([m_new] * (blk // LANES), axis=1))
            l_scr[half] = alpha * l_scr[half] + jnp.sum(p, axis=-1, keepdims=True)
            acc_scr[half] = (jnp.concatenate([alpha] * (DV_C // LANES), axis=1) * acc_scr[half]
                             + _dot(p.astype(BF16), vb))
            m_scr[half] = m_new

    produce(0, sa_scr)

    def pair(t, carry):
        produce(2 * t + 1, sb_scr)
        consume(2 * t, sa_scr, False)
        produce(2 * t + 2, sa_scr)
        consume(2 * t + 1, sb_scr, False)
        return carry

    lax.fori_loop(0, qi // 2, pair, 0)

    @pl.when(qi % 2 == 0)
    def _():
        consume(qi, sa_scr, True)

    @pl.when(qi % 2 == 1)
    def _():
        produce(qi, sb_scr)
        consume(qi - 1, sa_scr, False)
        consume(qi, sb_scr, True)

    o_ref[...] = _diff_out(acc_scr[0], l_scr[0][:, :1], acc_scr[1], l_scr[1][:, :1],
                           lam_ref[0, 0], g_ref[...], out_scale)


def _attn_prompt(q, k, v, lam, g, *, blk, out_scale):
    b, l, _ = q.shape
    kern = functools.partial(_attn_prompt_kernel, blk=blk, out_scale=out_scale)
    kv_spec = pl.BlockSpec((None, l, DV_C), lambda bi, hi, qi: (bi, 0, hi))
    return pl.pallas_call(
        kern,
        grid=(b, H_C, l // blk),
        in_specs=[pl.BlockSpec(memory_space=pltpu.SMEM),
                  pl.BlockSpec((None, blk, DV_C), lambda bi, hi, qi: (bi, qi, hi)),
                  kv_spec, kv_spec,
                  pl.BlockSpec((1, DV_C), lambda bi, hi, qi: (0, 0))],
        out_specs=pl.BlockSpec((None, blk, DV_C), lambda bi, hi, qi: (bi, qi, hi)),
        out_shape=jax.ShapeDtypeStruct((b, l, W_C), BF16),
        scratch_shapes=[pltpu.VMEM((2, blk, DV_C), F32),
                        pltpu.VMEM((2, blk, blk), F32),
                        pltpu.VMEM((2, blk, blk), F32),
                        pltpu.VMEM((2, blk, LANES), F32),
                        pltpu.VMEM((2, blk, LANES), F32)],
        compiler_params=_cparams(3),
        name="attn_prompt",
    )(lam, q, k, v, g)


def _cache_heads_kernel(c_ref, o_ref):
    for h in range(H_C):
        o_ref[:, h * DV_C:(h + 1) * DV_C] = c_ref[:, h, :].astype(BF16)


def _cache_heads(cache, *, tp):
    depth, b, past, _, _ = cache.shape
    return pl.pallas_call(
        _cache_heads_kernel,
        grid=(depth, b, past // tp),
        in_specs=[pl.BlockSpec((None, None, tp, H_C, DV_C), lambda d, bi, pi: (d, bi, pi, 0, 0))],
        out_specs=pl.BlockSpec((None, None, tp, W_C), lambda d, bi, pi: (d, bi, pi, 0)),
        out_shape=jax.ShapeDtypeStruct((depth, b, past, W_C), BF16),
        compiler_params=_cparams(3),
        name="cache_heads",
    )(cache)


def _attn_sample_kernel(lam_ref, q_ref, kn_ref, vn_ref, kp_ref, vp_ref, g_ref, o_ref, *, out_scale):
    q = q_ref[...]
    kn = kn_ref[...]
    vn = vn_ref[...]
    kp = kp_ref[...]
    vp = vp_ref[...]
    outs = []
    for half in range(2):
        hs = slice(half * DH_C, (half + 1) * DH_C)
        s_p = _dot_nt(q[:, hs], kp[:, hs])
        s_n = _dot_nt(q[:, hs], kn[:, hs])
        m = jnp.maximum(jnp.max(s_p, axis=-1, keepdims=True), jnp.max(s_n, axis=-1, keepdims=True))
        p_p = jnp.exp2(s_p - m)
        p_n = jnp.exp2(s_n - m)
        l = jnp.sum(p_p, axis=-1, keepdims=True) + jnp.sum(p_n, axis=-1, keepdims=True)
        outs += [_dot(p_p.astype(BF16), vp) + _dot(p_n.astype(BF16), vn), l]
    o_ref[...] = _diff_out(outs[0], outs[1], outs[2], outs[3], lam_ref[0, 0], g_ref[...], out_scale)


def _attn_sample(q, kn, vn, cache_k, cache_v, layer, lam, g, *, out_scale):
    b, n, _ = q.shape
    past = cache_k.shape[2]
    kern = functools.partial(_attn_sample_kernel, out_scale=out_scale)
    new_spec = pl.BlockSpec((None, n, DV_C), lambda bi, hi: (bi, 0, hi))
    past_spec = pl.BlockSpec((None, None, past, DV_C), lambda bi, hi: (layer, bi, 0, hi))
    return pl.pallas_call(
        kern,
        grid=(b, H_C),
        in_specs=[pl.BlockSpec(memory_space=pltpu.SMEM),
                  new_spec, new_spec, new_spec, past_spec, past_spec,
                  pl.BlockSpec((1, DV_C), lambda bi, hi: (0, 0))],
        out_specs=new_spec,
        out_shape=jax.ShapeDtypeStruct((b, n, W_C), BF16),
        compiler_params=_cparams(2),
        name="attn_sample",
    )(lam, q, kn, vn, cache_k, cache_v, g)


def _outproj_kernel(oa_ref, ob_ref, oc_ref, wa_ref, wb_ref, wc_ref, x_ref, gpost_ref, gpre_ref,
                    x1_ref, hm_ref):
    y = _dot(oa_ref[...], wa_ref[...]) + _dot(ob_ref[...], wb_ref[...]) + _dot(oc_ref[...], wc_ref[...])
    x1 = x_ref[...] + y * _rms_scale(y) * gpost_ref[...]
    x1_ref[...] = x1
    hm_ref[...] = (x1 * _rms_scale(x1) * gpre_ref[...]).astype(BF16)


def _outproj(oa, ob, oc, w_out, x, gpost, gpre, *, tm):
    t = x.shape[0]
    row = lambda w: pl.BlockSpec((tm, w), lambda i: (i, 0))
    vec = pl.BlockSpec((1, D_MODEL), lambda i: (0, 0))
    return pl.pallas_call(
        _outproj_kernel,
        grid=(t // tm,),
        in_specs=[row(W_A), row(W_B), row(W_C),
                  pl.BlockSpec((W_A, D_MODEL), lambda i: (0, 0)),
                  pl.BlockSpec((W_B, D_MODEL), lambda i: (1, 0)),
                  pl.BlockSpec((W_C, D_MODEL), lambda i: (1, 0)),
                  row(D_MODEL), vec, vec],
        out_specs=[row(D_MODEL), row(D_MODEL)],
        out_shape=[jax.ShapeDtypeStruct((t, D_MODEL), F32),
                   jax.ShapeDtypeStruct((t, D_MODEL), BF16)],
        compiler_params=_cparams(1),
        name="outproj",
    )(oa, ob, oc, w_out, w_out, w_out, x, gpost, gpre)


def _mlp_kernel(hm_ref, wu_ref, wd_ref, x1_ref, g_ref, o_ref, acc_scr):
    f = pl.program_id(1)

    @pl.when(f == 0)
    def _():
        acc_scr[...] = jnp.zeros_like(acc_scr)

    u = jnp.maximum(_dot(hm_ref[...], wu_ref[...]), 0.0)
    acc_scr[...] += _dot((u * u).astype(BF16), wd_ref[...])

    @pl.when(f == pl.num_programs(1) - 1)
    def _():
        y = acc_scr[...]
        o_ref[...] = x1_ref[...] + y * _rms_scale(y) * g_ref[...]


def _mlp(hm, w_up, w_down, x1, g, *, tm, tf):
    t = hm.shape[0]
    row = lambda dt: pl.BlockSpec((tm, D_MODEL), lambda i, f: (i, 0))
    return pl.pallas_call(
        _mlp_kernel,
        grid=(t // tm, D_FF // tf),
        in_specs=[row(BF16),
                  pl.BlockSpec((D_MODEL, tf), lambda i, f: (0, f)),
                  pl.BlockSpec((tf, D_MODEL), lambda i, f: (f, 0)),
                  row(F32),
                  pl.BlockSpec((1, D_MODEL), lambda i, f: (0, 0))],
        out_specs=row(F32),
        out_shape=jax.ShapeDtypeStruct((t, D_MODEL), F32),
        scratch_shapes=[pltpu.VMEM((tm, D_MODEL), F32)],
        compiler_params=_cparams(2),
        name="mlp",
    )(hm, w_up, w_down, x1, g)


def _rotary_tables(pos):
    inv = 1.0 / (RET_THETA_BASE ** jnp.linspace(0.0, 1.0, HALF_B, dtype=F32))
    ang = pos[:, None] * inv[None]
    return jnp.tile(jnp.cos(ang), (1, H_B)), jnp.tile(jnp.sin(ang), (1, H_B))


def _prep_w_in(w):
    def regroup(lo):
        return [w[:, lo + h * DK_B + half * HALF_B:lo + h * DK_B + (half + 1) * HALF_B]
                for half in range(2) for h in range(H_B)]

    a_end = 4 * H_A * DK_A
    wq = H_B * DK_B
    parts = [w[:, :a_end]] + regroup(a_end) + regroup(a_end + wq) + [w[:, a_end + 2 * wq:]]
    return jnp.concatenate(parts, axis=1).astype(BF16)


def _pick(n, pref):
    return pref if n % pref == 0 else n


def _layer(x, batch, s_a, s_b, rot, cache, rows_prev, layer, depth, lb, lam, lam_init, wts):
    w_in, w_out, w_up, w_down, g_pre_mix, g_post_mix, g_pre_mlp, g_post_mlp, g_hgrn, g_diff = wts
    t = x.shape[0]
    l = t // batch
    tm = _pick(t, 512)
    ab, q, kbf, vbf, k_rows, v_rows = _inproj(x, g_pre_mix, w_in, rows_prev, layer, depth,
                                              tm=_pick(t, 256), tn=512)
    ab3 = ab.reshape(batch, l, W_AB)
    oa, sa_new = _hgrn(ab3, s_a, lb, g_hgrn, first_layer=(layer == 0), tc=_pick(l, 512))
    ob, sb_new = _retention(ab3, s_b, rot[0], rot[1], tc=_pick(l, 256))
    q3, k3, v3 = (a.reshape(batch, l, W_CQ) for a in (q, kbf, vbf))
    out_scale = 1.0 - lam_init
    if cache is None:
        oc = _attn_prompt(q3, k3, v3, lam, g_diff, blk=512, out_scale=out_scale)
    else:
        oc = _attn_sample(q3, k3, v3, cache[0], cache[1], layer, lam, g_diff, out_scale=out_scale)
    x1, hm = _outproj(oa.reshape(t, W_A), ob.reshape(t, W_B), oc.reshape(t, W_C), w_out, x,
                      g_post_mix, g_pre_mlp, tm=tm)
    x2 = _mlp(hm, w_up, w_down, x1, g_post_mlp, tm=tm, tf=1024)
    return x2, sa_new, sb_new, (k_rows, v_rows)


def kernel(x_prompt, x_sample, state_hgrn, state_ret, cache_k, cache_v, w_in, w_out, w_up, w_down, g_pre_mix, g_post_mix, g_pre_mlp, g_post_mlp, hgrn_lb_logits, g_hgrn_out, lambda_q1, lambda_k1, lambda_q2, lambda_k2, g_diff_out):
    depth = w_in.shape[0]
    b_p, seq_p, _ = x_prompt.shape
    b_s, n_new, _ = x_sample.shape
    past_len = cache_k.shape[2]

    rot_p = _rotary_tables(jnp.arange(seq_p, dtype=F32))
    rot_s = _rotary_tables(past_len + jnp.arange(n_new, dtype=F32))
    lb_soft = jax.nn.softmax(hgrn_lb_logits.astype(F32), axis=0)
    lower_bounds = jnp.cumsum(lb_soft, axis=0) - lb_soft[0]
    cache = (_cache_heads(cache_k, tp=_pick(past_len, 512)), _cache_heads(cache_v, tp=_pick(past_len, 512)))
    sa0 = jnp.zeros((b_p, H_A, DK_A, DV_A), F32)
    sb0 = jnp.zeros((b_p, H_B, DK_B, DV_B), F32)

    hp = x_prompt.reshape(b_p * seq_p, D_MODEL)
    hs = x_sample.reshape(b_s * n_new, D_MODEL)
    states_p, states_s = [], []
    rows_p = rows_s = None
    for l in range(depth):
        lam_init = 0.8 - 0.6 * math.exp(-0.3 * l)
        lam = (jnp.exp(jnp.sum(lambda_q1[l] * lambda_k1[l])) - jnp.exp(jnp.sum(lambda_q2[l] * lambda_k2[l]))
               + lam_init).astype(F32).reshape(1, 1)
        vec = lambda a: a[l].reshape(1, -1)
        wts = (_prep_w_in(w_in[l]), w_out[l].astype(BF16), w_up[l].astype(BF16), w_down[l].astype(BF16),
               vec(g_pre_mix), vec(g_post_mix), vec(g_pre_mlp), vec(g_post_mlp), vec(g_hgrn_out),
               vec(g_diff_out))
        lb = lower_bounds[l].reshape(1, -1)
        hp, sa_p, sb_p, rows_p = _layer(hp, b_p, sa0, sb0, rot_p, None, rows_p, l, depth, lb, lam,
                                        lam_init, wts)
        hs, sa_s, sb_s, rows_s = _layer(hs, b_s, state_hgrn[l], state_ret[l], rot_s, cache, rows_s, l, depth,
                                        lb, lam, lam_init, wts)
        states_p.append((sa_p, sb_p))
        states_s.append((sa_s, sb_s))

    stack = lambda outs, i: jnp.stack([o[i] for o in outs])
    rows = lambda a, b, n: a.reshape(depth, b, n, H_C, DV_C)
    return (hp.reshape(b_p, seq_p, D_MODEL), hs.reshape(b_s, n_new, D_MODEL),
            stack(states_p, 0), stack(states_s, 0), stack(states_p, 1), stack(states_s, 1),
            rows(rows_p[0], b_p, seq_p), rows(rows_s[0], b_s, n_new),
            rows(rows_p[1], b_p, seq_p), rows(rows_s[1], b_s, n_new))
```

```python
import functools
import math

import jax
import jax.numpy as jnp
from jax import lax
from jax.experimental import pallas as pl
from jax.experimental.pallas import tpu as pltpu

F32 = jnp.float32
BF16 = jnp.bfloat16

D_MODEL = 2048
CHUNK = 64
EPS = 1e-6
H_A, DK_A, DV_A = 4, 128, 128
H_B, DK_B, DV_B = 4, 64, 128
RET_THETA_BASE = 10000.0
H_C, DH_C, DV_C = 4, 128, 256
D_FF = 4 * D_MODEL
W_A = H_A * DV_A
W_B = H_B * DV_B
W_C = H_C * DV_C
W_AB = 4 * H_A * DK_A + 2 * H_B * DK_B + 2 * H_B * DV_B
W_CQ = H_C * 2 * DH_C
IN_WIDTH = W_AB + 3 * W_CQ
HALF_B = DK_B // 2

VMEM_LIMIT_BYTES = 56 * 1024 * 1024
SUBCHUNK = 16
NEG_BIG = -1e30
LOG2_E = 1.4426950408889634
LANES = 128


def _cparams(n_axes):
    return pltpu.CompilerParams(dimension_semantics=("arbitrary",) * n_axes,
                                vmem_limit_bytes=VMEM_LIMIT_BYTES)


def _rms_scale(x):
    return lax.rsqrt(jnp.mean(x * x, axis=-1, keepdims=True) + EPS)


def _sigmoid(x):
    return 1.0 / (1.0 + jnp.exp(-x))


def _dot(a, b):
    return jnp.dot(a, b, preferred_element_type=F32)


def _dot_nt(a, b):
    return lax.dot_general(a, b, (((1,), (1,)), ((), ())), preferred_element_type=F32)


def _dot_tn(a, b):
    return lax.dot_general(a, b, (((0,), (0,)), ((), ())), preferred_element_type=F32)


def _inproj_kernel(*refs, q_scale, aliased, tn):
    x_ref, g_ref, w_ref = refs[:3]
    ab_ref, q_ref, kbf_ref, vbf_ref, krow_ref, vrow_ref, h_scr = refs[5 if aliased else 3:]
    x = x_ref[...]
    h_scr[...] = (x * _rms_scale(x) * g_ref[...]).astype(BF16)
    heads_per_group = tn // DV_C

    def cols(lo):
        return _dot(h_scr[...], w_ref[:, lo:lo + tn])

    for c in range(W_AB // tn):
        ab_ref[:, c * tn:(c + 1) * tn] = cols(c * tn)
    for c in range(W_CQ // tn):
        q_ref[:, c * tn:(c + 1) * tn] = (cols(W_AB + c * tn) * q_scale).astype(BF16)
    for group, (bf_ref, row_ref) in enumerate(((kbf_ref, krow_ref), (vbf_ref, vrow_ref))):
        for c in range(W_CQ // tn):
            acc = cols(W_AB + (1 + group) * W_CQ + c * tn)
            bf_ref[:, c * tn:(c + 1) * tn] = acc.astype(BF16)
            for hh in range(heads_per_group):
                row_ref[:, c * heads_per_group + hh, :] = acc[:, hh * DV_C:(hh + 1) * DV_C]


def _inproj(x, g, w, rows_prev, layer, depth, *, tm, tn):
    t = x.shape[0]
    aliased = rows_prev is not None
    kern = functools.partial(_inproj_kernel, q_scale=DH_C ** -0.5 * LOG2_E, aliased=aliased, tn=tn)
    row_spec = pl.BlockSpec((None, tm, H_C, DV_C), lambda i: (layer, i, 0, 0))
    row_shape = jax.ShapeDtypeStruct((depth, t, H_C, DV_C), F32)
    tok = lambda width: pl.BlockSpec((tm, width), lambda i: (i, 0))
    in_specs = [tok(D_MODEL),
                pl.BlockSpec((1, D_MODEL), lambda i: (0, 0)),
                pl.BlockSpec((None, D_MODEL, IN_WIDTH), lambda i: (layer, 0, 0),
                             pipeline_mode=pl.Buffered(1))]
    args = [x, g, w]
    if aliased:
        in_specs += [pl.BlockSpec(memory_space=pl.ANY)] * 2
        args += list(rows_prev)
    return pl.pallas_call(
        kern,
        grid=(t // tm,),
        in_specs=in_specs,
        out_specs=[tok(W_AB), tok(W_CQ), tok(W_CQ), tok(W_CQ), row_spec, row_spec],
        out_shape=[jax.ShapeDtypeStruct((t, W_AB), F32),
                   jax.ShapeDtypeStruct((t, W_CQ), BF16),
                   jax.ShapeDtypeStruct((t, W_CQ), BF16),
                   jax.ShapeDtypeStruct((t, W_CQ), BF16),
                   row_shape, row_shape],
        input_output_aliases={3: 4, 4: 5} if aliased else {},
        scratch_shapes=[pltpu.VMEM((tm, D_MODEL), BF16)],
        compiler_params=_cparams(1),
        name="inproj",
    )(*args)


def _hgrn_kernel(lb_ref, gout_ref, s0_ref, q_ref, f_ref, i_ref, g_ref, o_ref, sfin_ref, st_scr, row_scr,
                 *, first_layer, n_chunks, nb):
    t = pl.program_id(1)

    @pl.when(t == 0)
    def _():
        for bb in range(nb):
            for h in range(H_A):
                st_scr[bb, h] = s0_ref[bb, h].T

    rows = lax.broadcasted_iota(jnp.int32, (CHUNK, CHUNK), 0)
    cols = lax.broadcasted_iota(jnp.int32, (CHUNK, CHUNK), 1)
    tri = (cols <= rows).astype(BF16)
    sub_rows = lax.broadcasted_iota(jnp.int32, (SUBCHUNK, 1), 0)
    gout = gout_ref[...]
    lb = lb_ref[...]
    n_sub = CHUNK // SUBCHUNK

    def chunk_rows(bb, r0):
        zq = q_ref[bb, pl.ds(r0, CHUNK), :]
        zf = f_ref[bb, pl.ds(r0, CHUNK), :]
        vi = i_ref[bb, pl.ds(r0, CHUNK), :]
        zg = g_ref[bb, pl.ds(r0, CHUNK), :]

        e = jnp.exp(-jnp.abs(zf))
        r = 1.0 / (1.0 + e)
        pos = zf >= 0.0
        if first_layer:
            logf = jnp.minimum(zf, 0.0) - jnp.log(1.0 + e)
        else:
            logf = jnp.log(lb + (1.0 - lb) * jnp.where(pos, r, e * r))
        kk = (1.0 - lb) * jnp.where(pos, e * r, r)
        qs = zq * _sigmoid(zq) * (DK_A ** -0.5)
        gate = zg * _sigmoid(zg)

        g_hi = logf.astype(BF16)
        r1 = logf - g_hi.astype(F32)
        g_mid = r1.astype(BF16)
        g_lo = (r1 - g_mid.astype(F32)).astype(BF16)
        bcum = (_dot(tri, g_hi) + _dot(tri, g_mid) + _dot(tri, g_lo)) * LOG2_E
        row_scr[bb, 0] = bcum
        row_scr[bb, 1] = kk
        row_scr[bb, 2] = vi

        for h in range(H_A):
            sl = slice(h * DK_A, (h + 1) * DK_A)
            b = bcum[:, sl]
            q_h = qs[:, sl]
            k_h = kk[:, sl]
            v_bf = vi[:, sl].astype(BF16)
            st = st_scr[bb, h]
            btot = b[CHUNK - 1:CHUNK, :]

            o_inter = _dot_nt((q_h * jnp.exp2(b)).astype(BF16), st.astype(BF16))
            kdec = (k_h * jnp.exp2(btot - b)).astype(BF16)
            st_scr[bb, h] = st * jnp.exp2(btot) + _dot_tn(v_bf, kdec)

            for i in range(n_sub):
                lo = i * SUBCHUNK
                b_i = b[lo:lo + SUBCHUNK, :]
                q_i = q_h[lo:lo + SUBCHUNK, :]
                o_i = o_inter[lo:lo + SUBCHUNK, :]
                if i > 0:
                    ref_row = b[lo - 1:lo, :]
                    qd = (q_i * jnp.exp2(b_i - ref_row)).astype(BF16)
                    kd = (k_h[:lo, :] * jnp.exp2(ref_row - b[:lo, :])).astype(BF16)
                    a = _dot_nt(qd, kd)
                    o_i = o_i + _dot(a.astype(BF16), v_bf[:lo, :])
                for s in range(SUBCHUNK):
                    sg = lo + s
                    b_s = row_scr[bb, 0, sg:sg + 1, sl]
                    k_s = row_scr[bb, 1, sg:sg + 1, sl]
                    v_s = row_scr[bb, 2, sg:sg + 1, sl]
                    d = jnp.exp2(jnp.minimum(b_i - b_s, 0.0))
                    w = jnp.sum(q_i * d * k_s, axis=-1, keepdims=True)
                    w = jnp.where(sub_rows >= s, w, 0.0)
                    o_i = o_i + w * v_s
                y = o_i * _rms_scale(o_i) * gout * gate[lo:lo + SUBCHUNK, sl]
                o_ref[bb, pl.ds(r0 + lo, SUBCHUNK), sl] = y.astype(BF16)

    def chunk_body(c, carry):
        r0 = pl.multiple_of(c * CHUNK, CHUNK)
        for bb in range(nb):
            chunk_rows(bb, r0)
        return carry

    lax.fori_loop(0, n_chunks, chunk_body, 0)

    @pl.when(t == pl.num_programs(1) - 1)
    def _():
        for bb in range(nb):
            for h in range(H_A):
                sfin_ref[bb, h] = st_scr[bb, h].T


def _hgrn(ab, s0, lb, gout, *, first_layer, tc, nb):
    b, l, _ = ab.shape
    n_chunks = tc // CHUNK
    kern = functools.partial(_hgrn_kernel, first_layer=first_layer, n_chunks=n_chunks, nb=nb)
    blk = H_A * DK_A

    def col(k):
        return pl.BlockSpec((nb, tc, blk), lambda bi, ti: (bi, ti, k))

    state_spec = pl.BlockSpec((nb, H_A, DK_A, DV_A), lambda bi, ti: (bi, 0, 0, 0))
    return pl.pallas_call(
        kern,
        grid=(b // nb, l // tc),
        in_specs=[pl.BlockSpec((1, blk), lambda bi, ti: (0, 0)),
                  pl.BlockSpec((1, DV_A), lambda bi, ti: (0, 0)),
                  state_spec, col(0), col(1), col(2), col(3)],
        out_specs=[pl.BlockSpec((nb, tc, W_A), lambda bi, ti: (bi, ti, 0)), state_spec],
        out_shape=[jax.ShapeDtypeStruct((b, l, W_A), BF16),
                   jax.ShapeDtypeStruct((b, H_A, DK_A, DV_A), F32)],
        scratch_shapes=[pltpu.VMEM((nb, H_A, DV_A, DK_A), F32),
                        pltpu.VMEM((nb, 3, CHUNK, blk), F32)],
        compiler_params=_cparams(2),
        name="hgrn",
    )(lb, gout, s0, ab, ab, ab, ab)


_AB_RET_Q = (4 * H_A * DK_A) // (H_B * DK_B)
_AB_RET_V = (4 * H_A * DK_A + 2 * H_B * DK_B) // W_B
_LOG_GAMMA = tuple(math.log1p(-(2.0 ** (-5.0 - h))) for h in range(H_B))


def _ret_kernel(cos_ref, sin_ref, s0_ref, q_ref, k_ref, v_ref, g_ref, o_ref, sfin_ref, s_scr, *, tc):
    t = pl.program_id(1)
    wq = H_B * DK_B
    wh = H_B * HALF_B

    @pl.when(t == 0)
    def _():
        s_scr[...] = jnp.zeros_like(s_scr)
        for h in range(H_B):
            s_scr[h, h * HALF_B:(h + 1) * HALF_B, :] = s0_ref[h, :HALF_B, :]
            s_scr[h, wh + h * HALF_B:wh + (h + 1) * HALF_B, :] = s0_ref[h, HALF_B:, :]

    c = cos_ref[...]
    s = sin_ref[...]

    def rot(x):
        x1, x2 = x[:, :wh], x[:, wh:]
        return jnp.concatenate([x1 * c - x2 * s, x1 * s + x2 * c], axis=1)

    qr = rot(q_ref[...]).astype(BF16)
    kr = rot(k_ref[...]) * (DK_B ** -0.5)
    v = v_ref[...]
    gate_in = g_ref[...]

    lane_head = (lax.broadcasted_iota(jnp.int32, (1, wq), 1) % wh) // HALF_B
    rel = (lax.broadcasted_iota(jnp.int32, (tc, tc), 0)
           - lax.broadcasted_iota(jnp.int32, (tc, tc), 1))
    relf = jnp.maximum(rel, 0).astype(F32)
    tpos = lax.broadcasted_iota(jnp.int32, (tc, 1), 0).astype(F32)

    for h in range(H_B):
        lg = _LOG_GAMMA[h]
        sl = slice(h * DV_B, (h + 1) * DV_B)
        kh = jnp.where(lane_head == h, kr, 0.0)
        v_bf = v[:, sl].astype(BF16)
        dmat = jnp.where(rel >= 0, jnp.exp(relf * lg), 0.0)
        sc = _dot_nt(qr, kh.astype(BF16)) * dmat
        st = s_scr[h]
        o = _dot(sc.astype(BF16), v_bf) + _dot(qr, st.astype(BF16)) * jnp.exp((tpos + 1.0) * lg)
        ktail = (kh * jnp.exp((tc - 1.0 - tpos) * lg)).astype(BF16)
        s_scr[h] = math.exp(tc * lg) * st + _dot_tn(ktail, v_bf)
        zg = gate_in[:, sl]
        o_ref[:, sl] = (o * _rms_scale(o) * (zg * _sigmoid(zg))).astype(BF16)

    @pl.when(t == pl.num_programs(1) - 1)
    def _():
        for h in range(H_B):
            sfin_ref[h, :HALF_B, :] = s_scr[h, h * HALF_B:(h + 1) * HALF_B, :]
            sfin_ref[h, HALF_B:, :] = s_scr[h, wh + h * HALF_B:wh + (h + 1) * HALF_B, :]


def _retention(ab, s0, cos, sin, *, tc):
    b, l, _ = ab.shape
    wq = H_B * DK_B
    wh = H_B * HALF_B
    kern = functools.partial(_ret_kernel, tc=tc)
    state_spec = pl.BlockSpec((None, H_B, DK_B, DV_B), lambda bi, ti: (bi, 0, 0, 0))
    return pl.pallas_call(
        kern,
        grid=(b, l // tc),
        in_specs=[pl.BlockSpec((tc, wh), lambda bi, ti: (ti, 0)),
                  pl.BlockSpec((tc, wh), lambda bi, ti: (ti, 0)),
                  state_spec,
                  pl.BlockSpec((None, tc, wq), lambda bi, ti: (bi, ti, _AB_RET_Q)),
                  pl.BlockSpec((None, tc, wq), lambda bi, ti: (bi, ti, _AB_RET_Q + 1)),
                  pl.BlockSpec((None, tc, W_B), lambda bi, ti: (bi, ti, _AB_RET_V)),
                  pl.BlockSpec((None, tc, W_B), lambda bi, ti: (bi, ti, _AB_RET_V + 1))],
        out_specs=[pl.BlockSpec((None, tc, W_B), lambda bi, ti: (bi, ti, 0)), state_spec],
        out_shape=[jax.ShapeDtypeStruct((b, l, W_B), BF16),
                   jax.ShapeDtypeStruct((b, H_B, DK_B, DV_B), F32)],
        scratch_shapes=[pltpu.VMEM((H_B, wq, DV_B), F32)],
        compiler_params=_cparams(2),
        name="retention",
    )(cos, sin, s0, ab, ab, ab, ab)


def _diff_out(o1, l1, o2, l2, lam, g, out_scale):
    o = o1 / l1 - lam * (o2 / l2)
    return (o * _rms_scale(o) * g * out_scale).astype(BF16)


def _attn_prompt_kernel(lam_ref, q_ref, k_ref, v_ref, g_ref, o_ref, acc_scr, sa_scr, sb_scr, m_scr, l_scr,
                        *, blk, out_scale):
    qi = pl.program_id(2)
    q = q_ref[...]
    qh = (q[:, :DH_C], q[:, DH_C:])
    rq = lax.broadcasted_iota(jnp.int32, (blk, blk), 0) // CHUNK
    ck = lax.broadcasted_iota(jnp.int32, (blk, blk), 1) // CHUNK
    visible = ck <= rq
    acc_scr[...] = jnp.zeros_like(acc_scr)
    m_scr[...] = jnp.full(m_scr.shape, NEG_BIG, F32)
    l_scr[...] = jnp.zeros_like(l_scr)

    def produce(j, buf):
        off = pl.multiple_of(j * blk, blk)
        kb = k_ref[pl.ds(off, blk), :]
        for half in range(2):
            buf[half] = _dot_nt(qh[half], kb[:, half * DH_C:(half + 1) * DH_C])

    def consume(j, buf, masked):
        off = pl.multiple_of(j * blk, blk)
        vb = v_ref[pl.ds(off, blk), :]
        for half in range(2):
            m_old = m_scr[half]
            s = buf[half]
            if masked:
                s = jnp.where(visible, s, -jnp.inf)
            m_new = jnp.maximum(m_old, jnp.max(s, axis=-1, keepdims=True))
            alpha = jnp.exp2(m_old - m_new)
            p = jnp.exp2(s - jnp.concatenate([m_new] * (blk // LANES), axis=1))
            l_scr[half] = alpha * l_scr[half] + jnp.sum(p, axis=-1, keepdims=True)
            acc_scr[half] = (jnp.concatenate([alpha] * (DV_C // LANES), axis=1) * acc_scr[half]
                             + _dot(p.astype(BF16), vb))
            m_scr[half] = m_new

    produce(0, sa_scr)

    def pair(t, carry):
        produce(2 * t + 1, sb_scr)
        consume(2 * t, sa_scr, False)
        produce(2 * t + 2, sa_scr)
        consume(2 * t + 1, sb_scr, False)
        return carry

    lax.fori_loop(0, qi // 2, pair, 0)

    @pl.when(qi % 2 == 0)
    def _():
        consume(qi, sa_scr, True)

    @pl.when(qi % 2 == 1)
    def _():
        produce(qi, sb_scr)
        consume(qi - 1, sa_scr, False)
        consume(qi, sb_scr, True)

    o_ref[...] = _diff_out(acc_scr[0], l_scr[0][:, :1], acc_scr[1], l_scr[1][:, :1],
                           lam_ref[0, 0], g_ref[...], out_scale)


def _attn_prompt(q, k, v, lam, g, *, blk, out_scale):
    b, l, _ = q.shape
    kern = functools.partial(_attn_prompt_kernel, blk=blk, out_scale=out_scale)
    kv_spec = pl.BlockSpec((None, l, DV_C), lambda bi, hi, qi: (bi, 0, hi))
    return pl.pallas_call(
        kern,
        grid=(b, H_C, l // blk),
        in_specs=[pl.BlockSpec(memory_space=pltpu.SMEM),
                  pl.BlockSpec((None, blk, DV_C), lambda bi, hi, qi: (bi, qi, hi)),
                  kv_spec, kv_spec,
                  pl.BlockSpec((1, DV_C), lambda bi, hi, qi: (0, 0))],
        out_specs=pl.BlockSpec((None, blk, DV_C), lambda bi, hi, qi: (bi, qi, hi)),
        out_shape=jax.ShapeDtypeStruct((b, l, W_C), BF16),
        scratch_shapes=[pltpu.VMEM((2, blk, DV_C), F32),
                        pltpu.VMEM((2, blk, blk), F32),
                        pltpu.VMEM((2, blk, blk), F32),
                        pltpu.VMEM((2, blk, LANES), F32),
                        pltpu.VMEM((2, blk, LANES), F32)],
        compiler_params=_cparams(3),
        name="attn_prompt",
    )(lam, q, k, v, g)


def _cache_heads_kernel(c_ref, o_ref, dense_scr):
    for h in range(H_C):
        dense_scr[:, h * DV_C:(h + 1) * DV_C] = c_ref[:, h, :]
    o_ref[...] = dense_scr[...].astype(BF16)


def _cache_heads(cache, *, tp):
    depth, b, past, _, _ = cache.shape
    return pl.pallas_call(
        _cache_heads_kernel,
        grid=(depth, b, past // tp),
        in_specs=[pl.BlockSpec((None, None, tp, H_C, DV_C), lambda d, bi, pi: (d, bi, pi, 0, 0))],
        out_specs=pl.BlockSpec((None, None, tp, W_C), lambda d, bi, pi: (d, bi, pi, 0)),
        out_shape=jax.ShapeDtypeStruct((depth, b, past, W_C), BF16),
        scratch_shapes=[pltpu.VMEM((tp, W_C), F32)],
        compiler_params=_cparams(3),
        name="cache_heads",
    )(cache)


def _attn_sample_kernel(lam_ref, q_ref, kn_ref, vn_ref, kp_ref, vp_ref, g_ref, o_ref, *, out_scale):
    q = q_ref[...]
    kn = kn_ref[...]
    vn = vn_ref[...]
    kp = kp_ref[...]
    vp = vp_ref[...]
    outs = []
    for half in range(2):
        hs = slice(half * DH_C, (half + 1) * DH_C)
        s_p = _dot_nt(q[:, hs], kp[:, hs])
        s_n = _dot_nt(q[:, hs], kn[:, hs])
        m = jnp.maximum(jnp.max(s_p, axis=-1, keepdims=True), jnp.max(s_n, axis=-1, keepdims=True))
        p_p = jnp.exp2(s_p - m)
        p_n = jnp.exp2(s_n - m)
        l = jnp.sum(p_p, axis=-1, keepdims=True) + jnp.sum(p_n, axis=-1, keepdims=True)
        outs += [_dot(p_p.astype(BF16), vp) + _dot(p_n.astype(BF16), vn), l]
    o_ref[...] = _diff_out(outs[0], outs[1], outs[2], outs[3], lam_ref[0, 0], g_ref[...], out_scale)


def _attn_sample(q, kn, vn, cache_k, cache_v, layer, lam, g, *, out_scale):
    b, n, _ = q.shape
    past = cache_k.shape[2]
    kern = functools.partial(_attn_sample_kernel, out_scale=out_scale)
    new_spec = pl.BlockSpec((None, n, DV_C), lambda bi, hi: (bi, 0, hi))
    past_spec = pl.BlockSpec((None, None, past, DV_C), lambda bi, hi: (layer, bi, 0, hi))
    return pl.pallas_call(
        kern,
        grid=(b, H_C),
        in_specs=[pl.BlockSpec(memory_space=pltpu.SMEM),
                  new_spec, new_spec, new_spec, past_spec, past_spec,
                  pl.BlockSpec((1, DV_C), lambda bi, hi: (0, 0))],
        out_specs=new_spec,
        out_shape=jax.ShapeDtypeStruct((b, n, W_C), BF16),
        compiler_params=_cparams(2),
        name="attn_sample",
    )(lam, q, kn, vn, cache_k, cache_v, g)


def _outproj_kernel(oa_ref, ob_ref, oc_ref, wa_ref, wb_ref, wc_ref, x_ref, gpost_ref, gpre_ref,
                    x1_ref, hm_ref, *, n_split):
    rows = x_ref.shape[0] // n_split
    for r in range(n_split):
        rs = slice(r * rows, (r + 1) * rows)
        y = (_dot(oa_ref[rs, :], wa_ref[...]) + _dot(ob_ref[rs, :], wb_ref[...])
             + _dot(oc_ref[rs, :], wc_ref[...]))
        x1 = x_ref[rs, :] + y * _rms_scale(y) * gpost_ref[...]
        x1_ref[rs, :] = x1
        hm_ref[rs, :] = (x1 * _rms_scale(x1) * gpre_ref[...]).astype(BF16)


def _outproj(oa, ob, oc, w_out, layer, x, gpost, gpre, *, tm):
    t = x.shape[0]
    row = lambda w: pl.BlockSpec((tm, w), lambda i: (i, 0))
    vec = pl.BlockSpec((1, D_MODEL), lambda i: (0, 0))
    return pl.pallas_call(
        functools.partial(_outproj_kernel, n_split=2),
        grid=(t // tm,),
        in_specs=[row(W_A), row(W_B), row(W_C),
                  pl.BlockSpec((None, W_A, D_MODEL), lambda i: (layer, 0, 0)),
                  pl.BlockSpec((None, W_B, D_MODEL), lambda i: (layer, 1, 0)),
                  pl.BlockSpec((None, W_C, D_MODEL), lambda i: (layer, 1, 0)),
                  row(D_MODEL), vec, vec],
        out_specs=[row(D_MODEL), row(D_MODEL)],
        out_shape=[jax.ShapeDtypeStruct((t, D_MODEL), F32),
                   jax.ShapeDtypeStruct((t, D_MODEL), BF16)],
        compiler_params=_cparams(1),
        name="outproj",
    )(oa, ob, oc, w_out, w_out, w_out, x, gpost, gpre)


def _mlp_kernel(hm_ref, wu_ref, wd_ref, x1_ref, g_ref, o_ref, acc_scr):
    f = pl.program_id(1)

    @pl.when(f == 0)
    def _():
        acc_scr[...] = jnp.zeros_like(acc_scr)

    u = jnp.maximum(_dot(hm_ref[...], wu_ref[...]), 0.0)
    acc_scr[...] += _dot((u * u).astype(BF16), wd_ref[...])

    @pl.when(f == pl.num_programs(1) - 1)
    def _():
        y = acc_scr[...]
        o_ref[...] = x1_ref[...] + y * _rms_scale(y) * g_ref[...]


def _mlp(hm, w_up, w_down, layer, x1, g, *, tm, tf):
    t = hm.shape[0]
    row = lambda dt: pl.BlockSpec((tm, D_MODEL), lambda i, f: (i, 0))
    return pl.pallas_call(
        _mlp_kernel,
        grid=(t // tm, D_FF // tf),
        in_specs=[row(BF16),
                  pl.BlockSpec((None, D_MODEL, tf), lambda i, f: (layer, 0, f)),
                  pl.BlockSpec((None, tf, D_MODEL), lambda i, f: (layer, f, 0)),
                  row(F32),
                  pl.BlockSpec((1, D_MODEL), lambda i, f: (0, 0))],
        out_specs=row(F32),
        out_shape=jax.ShapeDtypeStruct((t, D_MODEL), F32),
        scratch_shapes=[pltpu.VMEM((tm, D_MODEL), F32)],
        compiler_params=_cparams(2),
        name="mlp",
    )(hm, w_up, w_down, x1, g)


def _rotary_tables(pos):
    inv = 1.0 / (RET_THETA_BASE ** jnp.linspace(0.0, 1.0, HALF_B, dtype=F32))
    ang = pos[:, None] * inv[None]
    return jnp.tile(jnp.cos(ang), (1, H_B)), jnp.tile(jnp.sin(ang), (1, H_B))


def _prep_w_in(w):
    def regroup(lo):
        return [w[..., lo + h * DK_B + half * HALF_B:lo + h * DK_B + (half + 1) * HALF_B]
                for half in range(2) for h in range(H_B)]

    a_end = 4 * H_A * DK_A
    wq = H_B * DK_B
    parts = [w[..., :a_end]] + regroup(a_end) + regroup(a_end + wq) + [w[..., a_end + 2 * wq:]]
    return jnp.concatenate(parts, axis=-1).astype(BF16)


def _pick(n, pref):
    return pref if n % pref == 0 else n


def _layer(x, batch, s_a, s_b, rot, cache, rows_prev, layer, depth, lb, lam, lam_init, wts):
    w_in, w_out, w_up, w_down, g_pre_mix, g_post_mix, g_pre_mlp, g_post_mlp, g_hgrn, g_diff = wts
    t = x.shape[0]
    l = t // batch
    tm = _pick(t, 512)
    ab, q, kbf, vbf, k_rows, v_rows = _inproj(x, g_pre_mix, w_in, rows_prev, layer, depth,
                                              tm=_pick(t, 256), tn=512)
    ab3 = ab.reshape(batch, l, W_AB)
    oa, sa_new = _hgrn(ab3, s_a, lb, g_hgrn, first_layer=(layer == 0), tc=_pick(l, 512),
                       nb=_pick(batch, 2))
    ob, sb_new = _retention(ab3, s_b, rot[0], rot[1], tc=_pick(l, 256))
    q3, k3, v3 = (a.reshape(batch, l, W_CQ) for a in (q, kbf, vbf))
    out_scale = 1.0 - lam_init
    if cache is None:
        oc = _attn_prompt(q3, k3, v3, lam, g_diff, blk=512, out_scale=out_scale)
    else:
        oc = _attn_sample(q3, k3, v3, cache[0], cache[1], layer, lam, g_diff, out_scale=out_scale)
    x1, hm = _outproj(oa.reshape(t, W_A), ob.reshape(t, W_B), oc.reshape(t, W_C), w_out, layer, x,
                      g_post_mix, g_pre_mlp, tm=tm)
    x2 = _mlp(hm, w_up, w_down, layer, x1, g_post_mlp, tm=tm, tf=1024)
    return x2, sa_new, sb_new, (k_rows, v_rows)


def kernel(x_prompt, x_sample, state_hgrn, state_ret, cache_k, cache_v, w_in, w_out, w_up, w_down, g_pre_mix, g_post_mix, g_pre_mlp, g_post_mlp, hgrn_lb_logits, g_hgrn_out, lambda_q1, lambda_k1, lambda_q2, lambda_k2, g_diff_out):
    depth = w_in.shape[0]
    b_p, seq_p, _ = x_prompt.shape
    b_s, n_new, _ = x_sample.shape
    past_len = cache_k.shape[2]

    rot_p = _rotary_tables(jnp.arange(seq_p, dtype=F32))
    rot_s = _rotary_tables(past_len + jnp.arange(n_new, dtype=F32))
    lb_soft = jax.nn.softmax(hgrn_lb_logits.astype(F32), axis=0)
    lower_bounds = jnp.cumsum(lb_soft, axis=0) - lb_soft[0]
    cache = (_cache_heads(cache_k, tp=_pick(past_len, 2048)), _cache_heads(cache_v, tp=_pick(past_len, 2048)))
    sa0 = jnp.zeros((b_p, H_A, DK_A, DV_A), F32)
    sb0 = jnp.zeros((b_p, H_B, DK_B, DV_B), F32)

    hp = x_prompt.reshape(b_p * seq_p, D_MODEL)
    hs = x_sample.reshape(b_s * n_new, D_MODEL)
    w_in_bf, w_out_bf, w_up_bf, w_down_bf = (_prep_w_in(w_in), w_out.astype(BF16), w_up.astype(BF16),
                                             w_down.astype(BF16))
    states_p, states_s = [], []
    rows_p = rows_s = None
    for l in range(depth):
        lam_init = 0.8 - 0.6 * math.exp(-0.3 * l)
        lam = (jnp.exp(jnp.sum(lambda_q1[l] * lambda_k1[l])) - jnp.exp(jnp.sum(lambda_q2[l] * lambda_k2[l]))
               + lam_init).astype(F32).reshape(1, 1)
        vec = lambda a: a[l].reshape(1, -1)
        wts = (w_in_bf, w_out_bf, w_up_bf, w_down_bf,
               vec(g_pre_mix), vec(g_post_mix), vec(g_pre_mlp), vec(g_post_mlp), vec(g_hgrn_out),
               vec(g_diff_out))
        lb = lower_bounds[l].reshape(1, -1)
        hp, sa_p, sb_p, rows_p = _layer(hp, b_p, sa0, sb0, rot_p, None, rows_p, l, depth, lb, lam,
                                        lam_init, wts)
        hs, sa_s, sb_s, rows_s = _layer(hs, b_s, state_hgrn[l], state_ret[l], rot_s, cache, rows_s, l, depth,
                                        lb, lam, lam_init, wts)
        states_p.append((sa_p, sb_p))
        states_s.append((sa_s, sb_s))

    stack = lambda outs, i: jnp.stack([o[i] for o in outs])
    rows = lambda a, b, n: a.reshape(depth, b, n, H_C, DV_C)
    return (hp.reshape(b_p, seq_p, D_MODEL), hs.reshape(b_s, n_new, D_MODEL),
            stack(states_p, 0), stack(states_s, 0), stack(states_p, 1), stack(states_s, 1),
            rows(rows_p[0], b_p, seq_p), rows(rows_s[0], b_s, n_new),
            rows(rows_p[1], b_p, seq_p), rows(rows_s[1], b_s, n_new))
```

```python
import functools
import math

import jax
import jax.numpy as jnp
from jax import lax
from jax.experimental import pallas as pl
from jax.experimental.pallas import tpu as pltpu

F32 = jnp.float32
BF16 = jnp.bfloat16

D_MODEL = 2048
CHUNK = 64
EPS = 1e-6
H_A, DK_A, DV_A = 4, 128, 128
H_B, DK_B, DV_B = 4, 64, 128
RET_THETA_BASE = 10000.0
H_C, DH_C, DV_C = 4, 128, 256
D_FF = 4 * D_MODEL
W_A = H_A * DV_A
W_B = H_B * DV_B
W_C = H_C * DV_C
W_AB = 4 * H_A * DK_A + 2 * H_B * DK_B + 2 * H_B * DV_B
W_CQ = H_C * 2 * DH_C
IN_WIDTH = W_AB + 3 * W_CQ
HALF_B = DK_B // 2

VMEM_LIMIT_BYTES = 56 * 1024 * 1024
SUBCHUNK = 16
NEG_BIG = -1e30
LOG2_E = 1.4426950408889634
LANES = 128


def _cparams(n_axes):
    return pltpu.CompilerParams(dimension_semantics=("arbitrary",) * n_axes,
                                vmem_limit_bytes=VMEM_LIMIT_BYTES)


def _rms_scale(x):
    return lax.rsqrt(jnp.mean(x * x, axis=-1, keepdims=True) + EPS)


def _sigmoid(x):
    return 1.0 / (1.0 + jnp.exp(-x))


def _dot(a, b):
    return jnp.dot(a, b, preferred_element_type=F32)


def _dot_nt(a, b):
    return lax.dot_general(a, b, (((1,), (1,)), ((), ())), preferred_element_type=F32)


def _dot_tn(a, b):
    return lax.dot_general(a, b, (((0,), (0,)), ((), ())), preferred_element_type=F32)


def _inproj_kernel(*refs, q_scale, aliased, tn):
    x_ref, g_ref, w_ref = refs[:3]
    ab_ref, q_ref, kbf_ref, vbf_ref, krow_ref, vrow_ref, h_scr = refs[5 if aliased else 3:]
    x = x_ref[...]
    h_scr[...] = (x * _rms_scale(x) * g_ref[...]).astype(BF16)
    heads_per_group = tn // DV_C

    def cols(lo):
        return _dot(h_scr[...], w_ref[:, lo:lo + tn])

    for c in range(W_AB // tn):
        ab_ref[:, c * tn:(c + 1) * tn] = cols(c * tn)
    for c in range(W_CQ // tn):
        q_ref[:, c * tn:(c + 1) * tn] = (cols(W_AB + c * tn) * q_scale).astype(BF16)
    for group, (bf_ref, row_ref) in enumerate(((kbf_ref, krow_ref), (vbf_ref, vrow_ref))):
        for c in range(W_CQ // tn):
            acc = cols(W_AB + (1 + group) * W_CQ + c * tn)
            bf_ref[:, c * tn:(c + 1) * tn] = acc.astype(BF16)
            for hh in range(heads_per_group):
                row_ref[:, c * heads_per_group + hh, :] = acc[:, hh * DV_C:(hh + 1) * DV_C]


def _inproj(x, g, w, rows_prev, layer, depth, *, tm, tn):
    t = x.shape[0]
    aliased = rows_prev is not None
    kern = functools.partial(_inproj_kernel, q_scale=DH_C ** -0.5 * LOG2_E, aliased=aliased, tn=tn)
    row_spec = pl.BlockSpec((None, tm, H_C, DV_C), lambda i: (layer, i, 0, 0))
    row_shape = jax.ShapeDtypeStruct((depth, t, H_C, DV_C), F32)
    tok = lambda width: pl.BlockSpec((tm, width), lambda i: (i, 0))
    in_specs = [tok(D_MODEL),
                pl.BlockSpec((1, D_MODEL), lambda i: (0, 0)),
                pl.BlockSpec((None, D_MODEL, IN_WIDTH), lambda i: (layer, 0, 0),
                             pipeline_mode=pl.Buffered(1))]
    args = [x, g, w]
    if aliased:
        in_specs += [pl.BlockSpec(memory_space=pl.ANY)] * 2
        args += list(rows_prev)
    return pl.pallas_call(
        kern,
        grid=(t // tm,),
        in_specs=in_specs,
        out_specs=[tok(W_AB), tok(W_CQ), tok(W_CQ), tok(W_CQ), row_spec, row_spec],
        out_shape=[jax.ShapeDtypeStruct((t, W_AB), F32),
                   jax.ShapeDtypeStruct((t, W_CQ), BF16),
                   jax.ShapeDtypeStruct((t, W_CQ), BF16),
                   jax.ShapeDtypeStruct((t, W_CQ), BF16),
                   row_shape, row_shape],
        input_output_aliases={3: 4, 4: 5} if aliased else {},
        scratch_shapes=[pltpu.VMEM((tm, D_MODEL), BF16)],
        compiler_params=_cparams(1),
        name="inproj",
    )(*args)


def _hgrn_kernel(lb_ref, gout_ref, s0_ref, q_ref, f_ref, i_ref, g_ref, o_ref, sfin_ref, st_scr, row_scr,
                 *, first_layer, n_chunks, nb):
    t = pl.program_id(1)

    @pl.when(t == 0)
    def _():
        for bb in range(nb):
            for h in range(H_A):
                st_scr[bb, h] = s0_ref[bb, h].T

    rows = lax.broadcasted_iota(jnp.int32, (CHUNK, CHUNK), 0)
    cols = lax.broadcasted_iota(jnp.int32, (CHUNK, CHUNK), 1)
    tri = (cols <= rows).astype(BF16)
    sub_rows = lax.broadcasted_iota(jnp.int32, (SUBCHUNK, 1), 0)
    gout = gout_ref[...]
    lb = lb_ref[...]
    n_sub = CHUNK // SUBCHUNK

    def chunk_rows(bb, r0):
        zq = q_ref[bb, pl.ds(r0, CHUNK), :]
        zf = f_ref[bb, pl.ds(r0, CHUNK), :]
        vi = i_ref[bb, pl.ds(r0, CHUNK), :]
        zg = g_ref[bb, pl.ds(r0, CHUNK), :]

        e = jnp.exp(-jnp.abs(zf))
        r = 1.0 / (1.0 + e)
        pos = zf >= 0.0
        if first_layer:
            logf = jnp.minimum(zf, 0.0) - jnp.log(1.0 + e)
        else:
            logf = jnp.log(lb + (1.0 - lb) * jnp.where(pos, r, e * r))
        kk = (1.0 - lb) * jnp.where(pos, e * r, r)
        qs = zq * _sigmoid(zq) * (DK_A ** -0.5)
        gate = zg * _sigmoid(zg)

        g_hi = logf.astype(BF16)
        r1 = logf - g_hi.astype(F32)
        g_mid = r1.astype(BF16)
        g_lo = (r1 - g_mid.astype(F32)).astype(BF16)
        bcum = (_dot(tri, g_hi) + _dot(tri, g_mid) + _dot(tri, g_lo)) * LOG2_E
        row_scr[bb, 0] = bcum
        row_scr[bb, 1] = kk
        row_scr[bb, 2] = vi

        for h in range(H_A):
            sl = slice(h * DK_A, (h + 1) * DK_A)
            b = bcum[:, sl]
            q_h = qs[:, sl]
            k_h = kk[:, sl]
            v_bf = vi[:, sl].astype(BF16)
            st = st_scr[bb, h]
            btot = b[CHUNK - 1:CHUNK, :]

            o_inter = _dot_nt((q_h * jnp.exp2(b)).astype(BF16), st.astype(BF16))
            kdec = (k_h * jnp.exp2(btot - b)).astype(BF16)
            st_scr[bb, h] = st * jnp.exp2(btot) + _dot_tn(v_bf, kdec)

            for i in range(n_sub):
                lo = i * SUBCHUNK
                b_i = b[lo:lo + SUBCHUNK, :]
                q_i = q_h[lo:lo + SUBCHUNK, :]
                o_i = o_inter[lo:lo + SUBCHUNK, :]
                if i > 0:
                    ref_row = b[lo - 1:lo, :]
                    qd = (q_i * jnp.exp2(b_i - ref_row)).astype(BF16)
                    kd = (k_h[:lo, :] * jnp.exp2(ref_row - b[:lo, :])).astype(BF16)
                    a = _dot_nt(qd, kd)
                    o_i = o_i + _dot(a.astype(BF16), v_bf[:lo, :])
                for s in range(SUBCHUNK):
                    sg = lo + s
                    b_s = row_scr[bb, 0, sg:sg + 1, sl]
                    k_s = row_scr[bb, 1, sg:sg + 1, sl]
                    v_s = row_scr[bb, 2, sg:sg + 1, sl]
                    d = jnp.exp2(jnp.minimum(b_i - b_s, 0.0))
                    w = jnp.sum(q_i * d * k_s, axis=-1, keepdims=True)
                    w = jnp.where(sub_rows >= s, w, 0.0)
                    o_i = o_i + w * v_s
                y = o_i * _rms_scale(o_i) * gout * gate[lo:lo + SUBCHUNK, sl]
                o_ref[bb, pl.ds(r0 + lo, SUBCHUNK), sl] = y.astype(BF16)

    def chunk_body(c, carry):
        r0 = pl.multiple_of(c * CHUNK, CHUNK)
        for bb in range(nb):
            chunk_rows(bb, r0)
        return carry

    lax.fori_loop(0, n_chunks, chunk_body, 0)

    @pl.when(t == pl.num_programs(1) - 1)
    def _():
        for bb in range(nb):
            for h in range(H_A):
                sfin_ref[bb, h] = st_scr[bb, h].T


def _hgrn(ab, s0, lb, gout, *, first_layer, tc, nb):
    b, l, _ = ab.shape
    n_chunks = tc // CHUNK
    kern = functools.partial(_hgrn_kernel, first_layer=first_layer, n_chunks=n_chunks, nb=nb)
    blk = H_A * DK_A

    def col(k):
        return pl.BlockSpec((nb, tc, blk), lambda bi, ti: (bi, ti, k))

    state_spec = pl.BlockSpec((nb, H_A, DK_A, DV_A), lambda bi, ti: (bi, 0, 0, 0))
    return pl.pallas_call(
        kern,
        grid=(b // nb, l // tc),
        in_specs=[pl.BlockSpec((1, blk), lambda bi, ti: (0, 0)),
                  pl.BlockSpec((1, DV_A), lambda bi, ti: (0, 0)),
                  state_spec, col(0), col(1), col(2), col(3)],
        out_specs=[pl.BlockSpec((nb, tc, W_A), lambda bi, ti: (bi, ti, 0)), state_spec],
        out_shape=[jax.ShapeDtypeStruct((b, l, W_A), BF16),
                   jax.ShapeDtypeStruct((b, H_A, DK_A, DV_A), F32)],
        scratch_shapes=[pltpu.VMEM((nb, H_A, DV_A, DK_A), F32),
                        pltpu.VMEM((nb, 3, CHUNK, blk), F32)],
        compiler_params=_cparams(2),
        name="hgrn",
    )(lb, gout, s0, ab, ab, ab, ab)


_AB_RET_Q = (4 * H_A * DK_A) // (H_B * DK_B)
_AB_RET_V = (4 * H_A * DK_A + 2 * H_B * DK_B) // W_B
_LOG_GAMMA = tuple(math.log1p(-(2.0 ** (-5.0 - h))) for h in range(H_B))


def _ret_kernel(cos_ref, sin_ref, s0_ref, q_ref, k_ref, v_ref, g_ref, o_ref, sfin_ref, s_scr, *, tc):
    t = pl.program_id(1)
    wq = H_B * DK_B
    wh = H_B * HALF_B

    @pl.when(t == 0)
    def _():
        s_scr[...] = jnp.zeros_like(s_scr)
        for h in range(H_B):
            s_scr[h, h * HALF_B:(h + 1) * HALF_B, :] = s0_ref[h, :HALF_B, :]
            s_scr[h, wh + h * HALF_B:wh + (h + 1) * HALF_B, :] = s0_ref[h, HALF_B:, :]

    c = cos_ref[...]
    s = sin_ref[...]

    def rot(x):
        x1, x2 = x[:, :wh], x[:, wh:]
        return jnp.concatenate([x1 * c - x2 * s, x1 * s + x2 * c], axis=1)

    qr = rot(q_ref[...]).astype(BF16)
    kr = rot(k_ref[...]) * (DK_B ** -0.5)
    v = v_ref[...]
    gate_in = g_ref[...]

    lane_head = (lax.broadcasted_iota(jnp.int32, (1, wq), 1) % wh) // HALF_B
    rel = (lax.broadcasted_iota(jnp.int32, (tc, tc), 0)
           - lax.broadcasted_iota(jnp.int32, (tc, tc), 1))
    relf = jnp.maximum(rel, 0).astype(F32)
    tpos = lax.broadcasted_iota(jnp.int32, (tc, 1), 0).astype(F32)

    for h in range(H_B):
        lg = _LOG_GAMMA[h]
        sl = slice(h * DV_B, (h + 1) * DV_B)
        kh = jnp.where(lane_head == h, kr, 0.0)
        v_bf = v[:, sl].astype(BF16)
        dmat = jnp.where(rel >= 0, jnp.exp(relf * lg), 0.0)
        sc = _dot_nt(qr, kh.astype(BF16)) * dmat
        st = s_scr[h]
        o = _dot(sc.astype(BF16), v_bf) + _dot(qr, st.astype(BF16)) * jnp.exp((tpos + 1.0) * lg)
        ktail = (kh * jnp.exp((tc - 1.0 - tpos) * lg)).astype(BF16)
        s_scr[h] = math.exp(tc * lg) * st + _dot_tn(ktail, v_bf)
        zg = gate_in[:, sl]
        o_ref[:, sl] = (o * _rms_scale(o) * (zg * _sigmoid(zg))).astype(BF16)

    @pl.when(t == pl.num_programs(1) - 1)
    def _():
        for h in range(H_B):
            sfin_ref[h, :HALF_B, :] = s_scr[h, h * HALF_B:(h + 1) * HALF_B, :]
            sfin_ref[h, HALF_B:, :] = s_scr[h, wh + h * HALF_B:wh + (h + 1) * HALF_B, :]


def _retention(ab, s0, cos, sin, *, tc):
    b, l, _ = ab.shape
    wq = H_B * DK_B
    wh = H_B * HALF_B
    kern = functools.partial(_ret_kernel, tc=tc)
    state_spec = pl.BlockSpec((None, H_B, DK_B, DV_B), lambda bi, ti: (bi, 0, 0, 0))
    return pl.pallas_call(
        kern,
        grid=(b, l // tc),
        in_specs=[pl.BlockSpec((tc, wh), lambda bi, ti: (ti, 0)),
                  pl.BlockSpec((tc, wh), lambda bi, ti: (ti, 0)),
                  state_spec,
                  pl.BlockSpec((None, tc, wq), lambda bi, ti: (bi, ti, _AB_RET_Q)),
                  pl.BlockSpec((None, tc, wq), lambda bi, ti: (bi, ti, _AB_RET_Q + 1)),
                  pl.BlockSpec((None, tc, W_B), lambda bi, ti: (bi, ti, _AB_RET_V)),
                  pl.BlockSpec((None, tc, W_B), lambda bi, ti: (bi, ti, _AB_RET_V + 1))],
        out_specs=[pl.BlockSpec((None, tc, W_B), lambda bi, ti: (bi, ti, 0)), state_spec],
        out_shape=[jax.ShapeDtypeStruct((b, l, W_B), BF16),
                   jax.ShapeDtypeStruct((b, H_B, DK_B, DV_B), F32)],
        scratch_shapes=[pltpu.VMEM((H_B, wq, DV_B), F32)],
        compiler_params=_cparams(2),
        name="retention",
    )(cos, sin, s0, ab, ab, ab, ab)


def _diff_out(o1, l1, o2, l2, lam, g, out_scale):
    o = o1 / l1 - lam * (o2 / l2)
    return (o * _rms_scale(o) * g * out_scale).astype(BF16)


def _attn_prompt_kernel(lam_ref, q_ref, k_ref, v_ref, g_ref, o_ref, acc_scr, sa_scr, sb_scr, m_scr, l_scr,
                        *, blk, out_scale):
    qi = pl.program_id(2)
    q = q_ref[...]
    qh = (q[:, :DH_C], q[:, DH_C:])
    rq = lax.broadcasted_iota(jnp.int32, (blk, blk), 0) // CHUNK
    ck = lax.broadcasted_iota(jnp.int32, (blk, blk), 1) // CHUNK
    visible = ck <= rq
    acc_scr[...] = jnp.zeros_like(acc_scr)
    m_scr[...] = jnp.full(m_scr.shape, NEG_BIG, F32)
    l_scr[...] = jnp.zeros_like(l_scr)

    def produce(j, buf):
        off = pl.multiple_of(j * blk, blk)
        kb = k_ref[pl.ds(off, blk), :]
        for half in range(2):
            buf[half] = _dot_nt(qh[half], kb[:, half * DH_C:(half + 1) * DH_C])

    def consume(j, buf, masked):
        off = pl.multiple_of(j * blk, blk)
        vb = v_ref[pl.ds(off, blk), :]
        for half in range(2):
            m_old = m_scr[half]
            s = buf[half]
            if masked:
                s = jnp.where(visible, s, -jnp.inf)
            m_new = jnp.maximum(m_old, jnp.max(s, axis=-1, keepdims=True))
            alpha = jnp.exp2(m_old - m_new)
            p = jnp.exp2(s - jnp.concatenate([m_new] * (blk // LANES), axis=1))
            l_scr[half] = alpha * l_scr[half] + jnp.sum(p, axis=-1, keepdims=True)
            acc_scr[half] = (jnp.concatenate([alpha] * (DV_C // LANES), axis=1) * acc_scr[half]
                             + _dot(p.astype(BF16), vb))
            m_scr[half] = m_new

    produce(0, sa_scr)

    def pair(t, carry):
        produce(2 * t + 1, sb_scr)
        consume(2 * t, sa_scr, False)
        produce(2 * t + 2, sa_scr)
        consume(2 * t + 1, sb_scr, False)
        return carry

    lax.fori_loop(0, qi // 2, pair, 0)

    @pl.when(qi % 2 == 0)
    def _():
        consume(qi, sa_scr, True)

    @pl.when(qi % 2 == 1)
    def _():
        produce(qi, sb_scr)
        consume(qi - 1, sa_scr, False)
        consume(qi, sb_scr, True)

    o_ref[...] = _diff_out(acc_scr[0], l_scr[0][:, :1], acc_scr[1], l_scr[1][:, :1],
                           lam_ref[0, 0], g_ref[...], out_scale)


def _attn_prompt(q, k, v, lam, g, *, blk, out_scale):
    b, l, _ = q.shape
    kern = functools.partial(_attn_prompt_kernel, blk=blk, out_scale=out_scale)
    kv_spec = pl.BlockSpec((None, l, DV_C), lambda bi, hi, qi: (bi, 0, hi))
    return pl.pallas_call(
        kern,
        grid=(b, H_C, l // blk),
        in_specs=[pl.BlockSpec(memory_space=pltpu.SMEM),
                  pl.BlockSpec((None, blk, DV_C), lambda bi, hi, qi: (bi, qi, hi)),
                  kv_spec, kv_spec,
                  pl.BlockSpec((1, DV_C), lambda bi, hi, qi: (0, 0))],
        out_specs=pl.BlockSpec((None, blk, DV_C), lambda bi, hi, qi: (bi, qi, hi)),
        out_shape=jax.ShapeDtypeStruct((b, l, W_C), BF16),
        scratch_shapes=[pltpu.VMEM((2, blk, DV_C), F32),
                        pltpu.VMEM((2, blk, blk), F32),
                        pltpu.VMEM((2, blk, blk), F32),
                        pltpu.VMEM((2, blk, LANES), F32),
                        pltpu.VMEM((2, blk, LANES), F32)],
        compiler_params=_cparams(3),
        name="attn_prompt",
    )(lam, q, k, v, g)


def _attn_sample_kernel(lam_ref, q_ref, kn_ref, vn_ref, k0_ref, k1_ref, v0_ref, v1_ref, g_ref, o_ref,
                        kp_scr, vp_scr, *, out_scale):
    past = kp_scr.shape[0]
    for dst, halves in ((kp_scr, (k0_ref, k1_ref)), (vp_scr, (v0_ref, v1_ref))):
        for c, c_ref in enumerate(halves):
            rows = c_ref.reshape(past * H_C, LANES)
            for h in range(H_C):
                lo = h * DV_C + c * LANES
                dst[:, lo:lo + LANES] = rows[pl.ds(h, past, stride=H_C), :].astype(BF16)

    lam = lam_ref[0, 0]
    g = g_ref[...]
    for h in range(H_C):
        hd = slice(h * DV_C, (h + 1) * DV_C)
        q = q_ref[:, hd]
        kn = kn_ref[:, hd]
        vn = vn_ref[:, hd]
        vp = vp_scr[:, hd]
        outs = []
        for half in range(2):
            hs = slice(half * DH_C, (half + 1) * DH_C)
            s_p = _dot_nt(q[:, hs], kp_scr[:, h * DV_C + half * DH_C:h * DV_C + (half + 1) * DH_C])
            s_n = _dot_nt(q[:, hs], kn[:, hs])
            m = jnp.maximum(jnp.max(s_p, axis=-1, keepdims=True), jnp.max(s_n, axis=-1, keepdims=True))
            p_p = jnp.exp2(s_p - m)
            p_n = jnp.exp2(s_n - m)
            l = jnp.sum(p_p, axis=-1, keepdims=True) + jnp.sum(p_n, axis=-1, keepdims=True)
            outs += [_dot(p_p.astype(BF16), vp) + _dot(p_n.astype(BF16), vn), l]
        o_ref[:, hd] = _diff_out(outs[0], outs[1], outs[2], outs[3], lam, g, out_scale)


def _attn_sample(q, kn, vn, cache_k, cache_v, layer, lam, g, *, out_scale):
    b, n, _ = q.shape
    past = cache_k.shape[2]
    kern = functools.partial(_attn_sample_kernel, out_scale=out_scale)
    new_spec = pl.BlockSpec((None, n, W_C), lambda bi: (bi, 0, 0))

    def past_spec(c):
        return pl.BlockSpec((None, None, past, H_C, LANES), lambda bi: (layer, bi, 0, 0, c))

    return pl.pallas_call(
        kern,
        grid=(b,),
        in_specs=[pl.BlockSpec(memory_space=pltpu.SMEM),
                  new_spec, new_spec, new_spec, past_spec(0), past_spec(1), past_spec(0), past_spec(1),
                  pl.BlockSpec((1, DV_C), lambda bi: (0, 0))],
        out_specs=new_spec,
        out_shape=jax.ShapeDtypeStruct((b, n, W_C), BF16),
        scratch_shapes=[pltpu.VMEM((past, W_C), BF16), pltpu.VMEM((past, W_C), BF16)],
        compiler_params=_cparams(1),
        name="attn_sample",
    )(lam, q, kn, vn, cache_k, cache_k, cache_v, cache_v, g)


def _outproj_kernel(oa_ref, ob_ref, oc_ref, wa_ref, wb_ref, wc_ref, x_ref, gpost_ref, gpre_ref,
                    x1_ref, hm_ref, *, n_split):
    rows = x_ref.shape[0] // n_split
    for r in range(n_split):
        rs = slice(r * rows, (r + 1) * rows)
        y = (_dot(oa_ref[rs, :], wa_ref[...]) + _dot(ob_ref[rs, :], wb_ref[...])
             + _dot(oc_ref[rs, :], wc_ref[...]))
        x1 = x_ref[rs, :] + y * _rms_scale(y) * gpost_ref[...]
        x1_ref[rs, :] = x1
        hm_ref[rs, :] = (x1 * _rms_scale(x1) * gpre_ref[...]).astype(BF16)


def _outproj(oa, ob, oc, w_out, layer, x, gpost, gpre, *, tm):
    t = x.shape[0]
    row = lambda w: pl.BlockSpec((tm, w), lambda i: (i, 0))
    vec = pl.BlockSpec((1, D_MODEL), lambda i: (0, 0))
    return pl.pallas_call(
        functools.partial(_outproj_kernel, n_split=2),
        grid=(t // tm,),
        in_specs=[row(W_A), row(W_B), row(W_C),
                  pl.BlockSpec((None, W_A, D_MODEL), lambda i: (layer, 0, 0)),
                  pl.BlockSpec((None, W_B, D_MODEL), lambda i: (layer, 1, 0)),
                  pl.BlockSpec((None, W_C, D_MODEL), lambda i: (layer, 1, 0)),
                  row(D_MODEL), vec, vec],
        out_specs=[row(D_MODEL), row(D_MODEL)],
        out_shape=[jax.ShapeDtypeStruct((t, D_MODEL), F32),
                   jax.ShapeDtypeStruct((t, D_MODEL), BF16)],
        compiler_params=_cparams(1),
        name="outproj",
    )(oa, ob, oc, w_out, w_out, w_out, x, gpost, gpre)


def _mlp_kernel(hm_ref, wu_ref, wd_ref, x1_ref, g_ref, o_ref, acc_scr):
    f = pl.program_id(1)

    @pl.when(f == 0)
    def _():
        acc_scr[...] = jnp.zeros_like(acc_scr)

    u = jnp.maximum(_dot(hm_ref[...], wu_ref[...]), 0.0)
    acc_scr[...] += _dot((u * u).astype(BF16), wd_ref[...])

    @pl.when(f == pl.num_programs(1) - 1)
    def _():
        y = acc_scr[...]
        o_ref[...] = x1_ref[...] + y * _rms_scale(y) * g_ref[...]


def _mlp(hm, w_up, w_down, layer, x1, g, *, tm, tf):
    t = hm.shape[0]
    row = lambda dt: pl.BlockSpec((tm, D_MODEL), lambda i, f: (i, 0))
    return pl.pallas_call(
        _mlp_kernel,
        grid=(t // tm, D_FF // tf),
        in_specs=[row(BF16),
                  pl.BlockSpec((None, D_MODEL, tf), lambda i, f: (layer, 0, f)),
                  pl.BlockSpec((None, tf, D_MODEL), lambda i, f: (layer, f, 0)),
                  row(F32),
                  pl.BlockSpec((1, D_MODEL), lambda i, f: (0, 0))],
        out_specs=row(F32),
        out_shape=jax.ShapeDtypeStruct((t, D_MODEL), F32),
        scratch_shapes=[pltpu.VMEM((tm, D_MODEL), F32)],
        compiler_params=_cparams(2),
        name="mlp",
    )(hm, w_up, w_down, x1, g)


def _rotary_tables(pos):
    inv = 1.0 / (RET_THETA_BASE ** jnp.linspace(0.0, 1.0, HALF_B, dtype=F32))
    ang = pos[:, None] * inv[None]
    return jnp.tile(jnp.cos(ang), (1, H_B)), jnp.tile(jnp.sin(ang), (1, H_B))


def _prep_w_in(w):
    def regroup(lo):
        return [w[..., lo + h * DK_B + half * HALF_B:lo + h * DK_B + (half + 1) * HALF_B]
                for half in range(2) for h in range(H_B)]

    a_end = 4 * H_A * DK_A
    wq = H_B * DK_B
    parts = [w[..., :a_end]] + regroup(a_end) + regroup(a_end + wq) + [w[..., a_end + 2 * wq:]]
    return jnp.concatenate([p.astype(BF16) for p in parts], axis=-1)


def _pick(n, pref):
    return pref if n % pref == 0 else n


def _layer(x, batch, s_a, s_b, rot, cache, rows_prev, layer, depth, lb, lam, lam_init, wts):
    w_in, w_out, w_up, w_down, g_pre_mix, g_post_mix, g_pre_mlp, g_post_mlp, g_hgrn, g_diff = wts
    t = x.shape[0]
    l = t // batch
    tm = _pick(t, 512)
    ab, q, kbf, vbf, k_rows, v_rows = _inproj(x, g_pre_mix, w_in, rows_prev, layer, depth,
                                              tm=_pick(t, 256), tn=512)
    ab3 = ab.reshape(batch, l, W_AB)
    oa, sa_new = _hgrn(ab3, s_a, lb, g_hgrn, first_layer=(layer == 0), tc=_pick(l, 512),
                       nb=_pick(batch, 2))
    ob, sb_new = _retention(ab3, s_b, rot[0], rot[1], tc=_pick(l, 256))
    q3, k3, v3 = (a.reshape(batch, l, W_CQ) for a in (q, kbf, vbf))
    out_scale = 1.0 - lam_init
    if cache is None:
        oc = _attn_prompt(q3, k3, v3, lam, g_diff, blk=512, out_scale=out_scale)
    else:
        oc = _attn_sample(q3, k3, v3, cache[0], cache[1], layer, lam, g_diff, out_scale=out_scale)
    x1, hm = _outproj(oa.reshape(t, W_A), ob.reshape(t, W_B), oc.reshape(t, W_C), w_out, layer, x,
                      g_post_mix, g_pre_mlp, tm=tm)
    x2 = _mlp(hm, w_up, w_down, layer, x1, g_post_mlp, tm=tm, tf=1024)
    return x2, sa_new, sb_new, (k_rows, v_rows)


def kernel(x_prompt, x_sample, state_hgrn, state_ret, cache_k, cache_v, w_in, w_out, w_up, w_down, g_pre_mix, g_post_mix, g_pre_mlp, g_post_mlp, hgrn_lb_logits, g_hgrn_out, lambda_q1, lambda_k1, lambda_q2, lambda_k2, g_diff_out):
    depth = w_in.shape[0]
    b_p, seq_p, _ = x_prompt.shape
    b_s, n_new, _ = x_sample.shape
    past_len = cache_k.shape[2]

    rot_p = _rotary_tables(jnp.arange(seq_p, dtype=F32))
    rot_s = _rotary_tables(past_len + jnp.arange(n_new, dtype=F32))
    lb_soft = jax.nn.softmax(hgrn_lb_logits.astype(F32), axis=0)
    lower_bounds = jnp.cumsum(lb_soft, axis=0) - lb_soft[0]
    cache = (cache_k, cache_v)
    sa0 = jnp.zeros((b_p, H_A, DK_A, DV_A), F32)
    sb0 = jnp.zeros((b_p, H_B, DK_B, DV_B), F32)

    hp = x_prompt.reshape(b_p * seq_p, D_MODEL)
    hs = x_sample.reshape(b_s * n_new, D_MODEL)
    w_in_bf, w_out_bf, w_up_bf, w_down_bf = (_prep_w_in(w_in), w_out.astype(BF16), w_up.astype(BF16),
                                             w_down.astype(BF16))
    states_p, states_s = [], []
    rows_p = rows_s = None
    for l in range(depth):
        lam_init = 0.8 - 0.6 * math.exp(-0.3 * l)
        lam = (jnp.exp(jnp.sum(lambda_q1[l] * lambda_k1[l])) - jnp.exp(jnp.sum(lambda_q2[l] * lambda_k2[l]))
               + lam_init).astype(F32).reshape(1, 1)
        vec = lambda a: a[l].reshape(1, -1)
        wts = (w_in_bf, w_out_bf, w_up_bf, w_down_bf,
               vec(g_pre_mix), vec(g_post_mix), vec(g_pre_mlp), vec(g_post_mlp), vec(g_hgrn_out),
               vec(g_diff_out))
        lb = lower_bounds[l].reshape(1, -1)
        hp, sa_p, sb_p, rows_p = _layer(hp, b_p, sa0, sb0, rot_p, None, rows_p, l, depth, lb, lam,
                                        lam_init, wts)
        hs, sa_s, sb_s, rows_s = _layer(hs, b_s, state_hgrn[l], state_ret[l], rot_s, cache, rows_s, l, depth,
                                        lb, lam, lam_init, wts)
        states_p.append((sa_p, sb_p))
        states_s.append((sa_s, sb_s))

    stack = lambda outs, i: jnp.stack([o[i] for o in outs])
    rows = lambda a, b, n: a.reshape(depth, b, n, H_C, DV_C)
    return (hp.reshape(b_p, seq_p, D_MODEL), hs.reshape(b_s, n_new, D_MODEL),
            stack(states_p, 0), stack(states_s, 0), stack(states_p, 1), stack(states_s, 1),
            rows(rows_p[0], b_p, seq_p), rows(rows_s[0], b_s, n_new),
            rows(rows_p[1], b_p, seq_p), rows(rows_s[1], b_s, n_new))
```

```python
import functools
import math

import jax
import jax.numpy as jnp
from jax import lax
from jax.experimental import pallas as pl
from jax.experimental.pallas import tpu as pltpu

F32 = jnp.float32
BF16 = jnp.bfloat16

D_MODEL = 2048
CHUNK = 64
EPS = 1e-6
H_A, DK_A, DV_A = 4, 128, 128
H_B, DK_B, DV_B = 4, 64, 128
RET_THETA_BASE = 10000.0
H_C, DH_C, DV_C = 4, 128, 256
D_FF = 4 * D_MODEL
W_A = H_A * DV_A
W_B = H_B * DV_B
W_C = H_C * DV_C
W_AB = 4 * H_A * DK_A + 2 * H_B * DK_B + 2 * H_B * DV_B
W_CQ = H_C * 2 * DH_C
IN_WIDTH = W_AB + 3 * W_CQ
HALF_B = DK_B // 2
RET_QK_LO = 4 * H_A * DK_A
RET_QK_W = 2 * H_B * DK_B

VMEM_LIMIT_BYTES = 56 * 1024 * 1024
SUBCHUNK = 16
NEG_BIG = -1e30
LOG2_E = 1.4426950408889634
LANES = 128


def _cparams(n_axes):
    return pltpu.CompilerParams(dimension_semantics=("arbitrary",) * n_axes,
                                vmem_limit_bytes=VMEM_LIMIT_BYTES)


def _rms_scale(x):
    return lax.rsqrt(jnp.mean(x * x, axis=-1, keepdims=True) + EPS)


def _sigmoid(x):
    return 1.0 / (1.0 + jnp.exp(-x))


def _dot(a, b):
    return jnp.dot(a, b, preferred_element_type=F32)


def _dot_nt(a, b):
    return lax.dot_general(a, b, (((1,), (1,)), ((), ())), preferred_element_type=F32)


def _dot_tn(a, b):
    return lax.dot_general(a, b, (((0,), (0,)), ((), ())), preferred_element_type=F32)


def _inproj_kernel(*refs, q_scale, aliased, tn):
    x_ref, g_ref, w_ref, wret_ref = refs[:4]
    ab_ref, q_ref, kbf_ref, vbf_ref, krow_ref, vrow_ref, h_scr = refs[6 if aliased else 4:]
    x = x_ref[...]
    h_scr[...] = (x * _rms_scale(x) * g_ref[...]).astype(BF16)
    heads_per_group = tn // DV_C

    def cols(lo):
        w = wret_ref[...] if lo == RET_QK_LO else w_ref[:, lo:lo + tn]
        return _dot(h_scr[...], w)

    for c in range(W_AB // tn):
        ab_ref[:, c * tn:(c + 1) * tn] = cols(c * tn)
    for c in range(W_CQ // tn):
        q_ref[:, c * tn:(c + 1) * tn] = (cols(W_AB + c * tn) * q_scale).astype(BF16)
    for group, (bf_ref, row_ref) in enumerate(((kbf_ref, krow_ref), (vbf_ref, vrow_ref))):
        for c in range(W_CQ // tn):
            acc = cols(W_AB + (1 + group) * W_CQ + c * tn)
            bf_ref[:, c * tn:(c + 1) * tn] = acc.astype(BF16)
            for hh in range(heads_per_group):
                row_ref[:, c * heads_per_group + hh, :] = acc[:, hh * DV_C:(hh + 1) * DV_C]


def _inproj(x, g, w, rows_prev, layer, depth, *, tm, tn):
    t = x.shape[0]
    w_all, w_ret = w
    assert tn == RET_QK_W and RET_QK_LO % tn == 0
    aliased = rows_prev is not None
    kern = functools.partial(_inproj_kernel, q_scale=DH_C ** -0.5 * LOG2_E, aliased=aliased, tn=tn)
    row_spec = pl.BlockSpec((None, tm, H_C, DV_C), lambda i: (layer, i, 0, 0))
    row_shape = jax.ShapeDtypeStruct((depth, t, H_C, DV_C), F32)
    tok = lambda width: pl.BlockSpec((tm, width), lambda i: (i, 0))
    in_specs = [tok(D_MODEL),
                pl.BlockSpec((1, D_MODEL), lambda i: (0, 0)),
                pl.BlockSpec((None, D_MODEL, IN_WIDTH), lambda i: (layer, 0, 0),
                             pipeline_mode=pl.Buffered(1)),
                pl.BlockSpec((None, D_MODEL, RET_QK_W), lambda i: (layer, 0, 0),
                             pipeline_mode=pl.Buffered(1))]
    args = [x, g, w_all, w_ret]
    if aliased:
        in_specs += [pl.BlockSpec(memory_space=pl.ANY)] * 2
        args += list(rows_prev)
    return pl.pallas_call(
        kern,
        grid=(t // tm,),
        in_specs=in_specs,
        out_specs=[tok(W_AB), tok(W_CQ), tok(W_CQ), tok(W_CQ), row_spec, row_spec],
        out_shape=[jax.ShapeDtypeStruct((t, W_AB), F32),
                   jax.ShapeDtypeStruct((t, W_CQ), BF16),
                   jax.ShapeDtypeStruct((t, W_CQ), BF16),
                   jax.ShapeDtypeStruct((t, W_CQ), BF16),
                   row_shape, row_shape],
        input_output_aliases={4: 4, 5: 5} if aliased else {},
        scratch_shapes=[pltpu.VMEM((tm, D_MODEL), BF16)],
        compiler_params=_cparams(1),
        name="inproj",
    )(*args)


def _hgrn_kernel(lb_ref, gout_ref, s0_ref, q_ref, f_ref, i_ref, g_ref, o_ref, sfin_ref, st_scr, row_scr,
                 *, first_layer, n_chunks, nb):
    t = pl.program_id(1)

    @pl.when(t == 0)
    def _():
        for bb in range(nb):
            for h in range(H_A):
                st_scr[bb, h] = s0_ref[bb, h].T

    rows = lax.broadcasted_iota(jnp.int32, (CHUNK, CHUNK), 0)
    cols = lax.broadcasted_iota(jnp.int32, (CHUNK, CHUNK), 1)
    tri = (cols <= rows).astype(BF16)
    sub_rows = lax.broadcasted_iota(jnp.int32, (SUBCHUNK, 1), 0)
    gout = gout_ref[...]
    lb = lb_ref[...]
    n_sub = CHUNK // SUBCHUNK

    def chunk_rows(bb, r0):
        zq = q_ref[bb, pl.ds(r0, CHUNK), :]
        zf = f_ref[bb, pl.ds(r0, CHUNK), :]
        vi = i_ref[bb, pl.ds(r0, CHUNK), :]
        zg = g_ref[bb, pl.ds(r0, CHUNK), :]

        e = jnp.exp(-jnp.abs(zf))
        r = 1.0 / (1.0 + e)
        pos = zf >= 0.0
        if first_layer:
            logf = jnp.minimum(zf, 0.0) - jnp.log(1.0 + e)
        else:
            logf = jnp.log(lb + (1.0 - lb) * jnp.where(pos, r, e * r))
        kk = (1.0 - lb) * jnp.where(pos, e * r, r)
        qs = zq * _sigmoid(zq) * (DK_A ** -0.5)
        gate = zg * _sigmoid(zg)

        g_hi = logf.astype(BF16)
        r1 = logf - g_hi.astype(F32)
        g_mid = r1.astype(BF16)
        g_lo = (r1 - g_mid.astype(F32)).astype(BF16)
        bcum = (_dot(tri, g_hi) + _dot(tri, g_mid) + _dot(tri, g_lo)) * LOG2_E
        row_scr[bb, 0] = bcum
        row_scr[bb, 1] = kk
        row_scr[bb, 2] = vi

        for h in range(H_A):
            sl = slice(h * DK_A, (h + 1) * DK_A)
            b = bcum[:, sl]
            q_h = qs[:, sl]
            k_h = kk[:, sl]
            v_bf = vi[:, sl].astype(BF16)
            st = st_scr[bb, h]
            btot = b[CHUNK - 1:CHUNK, :]

            o_inter = _dot_nt((q_h * jnp.exp2(b)).astype(BF16), st.astype(BF16))
            kdec = (k_h * jnp.exp2(btot - b)).astype(BF16)
            st_scr[bb, h] = st * jnp.exp2(btot) + _dot_tn(v_bf, kdec)

            for i in range(n_sub):
                lo = i * SUBCHUNK
                b_i = b[lo:lo + SUBCHUNK, :]
                q_i = q_h[lo:lo + SUBCHUNK, :]
                o_i = o_inter[lo:lo + SUBCHUNK, :]
                if i > 0:
                    ref_row = b[lo - 1:lo, :]
                    qd = (q_i * jnp.exp2(b_i - ref_row)).astype(BF16)
                    kd = (k_h[:lo, :] * jnp.exp2(ref_row - b[:lo, :])).astype(BF16)
                    a = _dot_nt(qd, kd)
                    o_i = o_i + _dot(a.astype(BF16), v_bf[:lo, :])
                for s in range(SUBCHUNK):
                    sg = lo + s
                    b_s = row_scr[bb, 0, sg:sg + 1, sl]
                    k_s = row_scr[bb, 1, sg:sg + 1, sl]
                    v_s = row_scr[bb, 2, sg:sg + 1, sl]
                    d = jnp.exp2(b_i - b_s)
                    w = jnp.sum(q_i * d * k_s, axis=-1, keepdims=True)
                    w = jnp.where(sub_rows >= s, w, 0.0)
                    o_i = o_i + w * v_s
                y = o_i * _rms_scale(o_i) * gout * gate[lo:lo + SUBCHUNK, sl]
                o_ref[bb, pl.ds(r0 + lo, SUBCHUNK), sl] = y.astype(BF16)

    def chunk_body(c, carry):
        r0 = pl.multiple_of(c * CHUNK, CHUNK)
        for bb in range(nb):
            chunk_rows(bb, r0)
        return carry

    lax.fori_loop(0, n_chunks, chunk_body, 0)

    @pl.when(t == pl.num_programs(1) - 1)
    def _():
        for bb in range(nb):
            for h in range(H_A):
                sfin_ref[bb, h] = st_scr[bb, h].T


def _hgrn(ab, s0, lb, gout, *, first_layer, tc, nb):
    b, l, _ = ab.shape
    n_chunks = tc // CHUNK
    kern = functools.partial(_hgrn_kernel, first_layer=first_layer, n_chunks=n_chunks, nb=nb)
    blk = H_A * DK_A

    def col(k):
        return pl.BlockSpec((nb, tc, blk), lambda bi, ti: (bi, ti, k))

    state_spec = pl.BlockSpec((nb, H_A, DK_A, DV_A), lambda bi, ti: (bi, 0, 0, 0))
    return pl.pallas_call(
        kern,
        grid=(b // nb, l // tc),
        in_specs=[pl.BlockSpec((1, blk), lambda bi, ti: (0, 0)),
                  pl.BlockSpec((1, DV_A), lambda bi, ti: (0, 0)),
                  state_spec, col(0), col(1), col(2), col(3)],
        out_specs=[pl.BlockSpec((nb, tc, W_A), lambda bi, ti: (bi, ti, 0)), state_spec],
        out_shape=[jax.ShapeDtypeStruct((b, l, W_A), BF16),
                   jax.ShapeDtypeStruct((b, H_A, DK_A, DV_A), F32)],
        scratch_shapes=[pltpu.VMEM((nb, H_A, DV_A, DK_A), F32),
                        pltpu.VMEM((nb, 3, CHUNK, blk), F32)],
        compiler_params=_cparams(2),
        name="hgrn",
    )(lb, gout, s0, ab, ab, ab, ab)


_AB_RET_Q = (4 * H_A * DK_A) // (H_B * DK_B)
_AB_RET_V = (4 * H_A * DK_A + 2 * H_B * DK_B) // W_B
_LOG_GAMMA = tuple(math.log1p(-(2.0 ** (-5.0 - h))) for h in range(H_B))


def _ret_kernel(cos_ref, sin_ref, s0_ref, q_ref, k_ref, v_ref, g_ref, o_ref, sfin_ref, s_scr, *, tc, nb):
    t = pl.program_id(1)
    wq = H_B * DK_B
    wh = H_B * HALF_B

    @pl.when(t == 0)
    def _():
        s_scr[...] = jnp.zeros_like(s_scr)
        for bb in range(nb):
            for h in range(H_B):
                s_scr[bb, h, h * HALF_B:(h + 1) * HALF_B, :] = s0_ref[bb, h, :HALF_B, :]
                s_scr[bb, h, wh + h * HALF_B:wh + (h + 1) * HALF_B, :] = s0_ref[bb, h, HALF_B:, :]

    c = cos_ref[...]
    s = sin_ref[...]

    def rot(x):
        x1, x2 = x[:, :wh], x[:, wh:]
        return jnp.concatenate([x1 * c - x2 * s, x1 * s + x2 * c], axis=1)

    lane_head = (lax.broadcasted_iota(jnp.int32, (1, wq), 1) % wh) // HALF_B
    rel = (lax.broadcasted_iota(jnp.int32, (tc, tc), 0)
           - lax.broadcasted_iota(jnp.int32, (tc, tc), 1))
    relf = jnp.maximum(rel, 0).astype(F32)
    tpos = lax.broadcasted_iota(jnp.int32, (tc, 1), 0).astype(F32)
    dmats = [jnp.where(rel >= 0, jnp.exp(relf * lg), 0.0) for lg in _LOG_GAMMA]
    inner = [jnp.exp((tpos + 1.0) * lg) for lg in _LOG_GAMMA]
    tail = [jnp.exp((tc - 1.0 - tpos) * lg) for lg in _LOG_GAMMA]

    for bb in range(nb):
        qr = rot(q_ref[bb]).astype(BF16)
        kr = rot(k_ref[bb]) * (DK_B ** -0.5)
        v = v_ref[bb]
        gate_in = g_ref[bb]
        for h in range(H_B):
            sl = slice(h * DV_B, (h + 1) * DV_B)
            kh = jnp.where(lane_head == h, kr, 0.0)
            v_bf = v[:, sl].astype(BF16)
            sc = _dot_nt(qr, kh.astype(BF16)) * dmats[h]
            st = s_scr[bb, h]
            o = _dot(sc.astype(BF16), v_bf) + _dot(qr, st.astype(BF16)) * inner[h]
            ktail = (kh * tail[h]).astype(BF16)
            s_scr[bb, h] = math.exp(tc * _LOG_GAMMA[h]) * st + _dot_tn(ktail, v_bf)
            zg = gate_in[:, sl]
            o_ref[bb, :, sl] = (o * _rms_scale(o) * (zg * _sigmoid(zg))).astype(BF16)

    @pl.when(t == pl.num_programs(1) - 1)
    def _():
        for bb in range(nb):
            for h in range(H_B):
                sfin_ref[bb, h, :HALF_B, :] = s_scr[bb, h, h * HALF_B:(h + 1) * HALF_B, :]
                sfin_ref[bb, h, HALF_B:, :] = s_scr[bb, h, wh + h * HALF_B:wh + (h + 1) * HALF_B, :]


def _retention(ab, s0, cos, sin, *, tc, nb):
    b, l, _ = ab.shape
    wq = H_B * DK_B
    wh = H_B * HALF_B
    kern = functools.partial(_ret_kernel, tc=tc, nb=nb)
    state_spec = pl.BlockSpec((nb, H_B, DK_B, DV_B), lambda bi, ti: (bi, 0, 0, 0))
    return pl.pallas_call(
        kern,
        grid=(b // nb, l // tc),
        in_specs=[pl.BlockSpec((tc, wh), lambda bi, ti: (ti, 0)),
                  pl.BlockSpec((tc, wh), lambda bi, ti: (ti, 0)),
                  state_spec,
                  pl.BlockSpec((nb, tc, wq), lambda bi, ti: (bi, ti, _AB_RET_Q)),
                  pl.BlockSpec((nb, tc, wq), lambda bi, ti: (bi, ti, _AB_RET_Q + 1)),
                  pl.BlockSpec((nb, tc, W_B), lambda bi, ti: (bi, ti, _AB_RET_V)),
                  pl.BlockSpec((nb, tc, W_B), lambda bi, ti: (bi, ti, _AB_RET_V + 1))],
        out_specs=[pl.BlockSpec((nb, tc, W_B), lambda bi, ti: (bi, ti, 0)), state_spec],
        out_shape=[jax.ShapeDtypeStruct((b, l, W_B), BF16),
                   jax.ShapeDtypeStruct((b, H_B, DK_B, DV_B), F32)],
        scratch_shapes=[pltpu.VMEM((nb, H_B, wq, DV_B), F32)],
        compiler_params=_cparams(2),
        name="retention",
    )(cos, sin, s0, ab, ab, ab, ab)


def _diff_out(o1, l1, o2, l2, lam, g, out_scale):
    o = o1 / l1 - lam * (o2 / l2)
    return (o * _rms_scale(o) * g * out_scale).astype(BF16)


def _attn_prompt_kernel(lam_ref, q_ref, k_ref, v_ref, g_ref, o_ref, acc_scr, sa_scr, sb_scr, m_scr, l_scr,
                        *, blk, out_scale):
    qi = pl.program_id(2)
    q = q_ref[...]
    qh = (q[:, :DH_C], q[:, DH_C:])
    rq = lax.broadcasted_iota(jnp.int32, (blk, blk), 0) // CHUNK
    ck = lax.broadcasted_iota(jnp.int32, (blk, blk), 1) // CHUNK
    visible = ck <= rq
    acc_scr[...] = jnp.zeros_like(acc_scr)
    m_scr[...] = jnp.full(m_scr.shape, NEG_BIG, F32)
    l_scr[...] = jnp.zeros_like(l_scr)

    def produce(j, buf):
        off = pl.multiple_of(j * blk, blk)
        kb = k_ref[pl.ds(off, blk), :]
        for half in range(2):
            buf[half] = _dot_nt(qh[half], kb[:, half * DH_C:(half + 1) * DH_C])

    def consume(j, buf, masked):
        off = pl.multiple_of(j * blk, blk)
        vb = v_ref[pl.ds(off, blk), :]
        for half in range(2):
            m_old = m_scr[half]
            s = buf[half]
            if masked:
                s = jnp.where(visible, s, -jnp.inf)
            m_new = jnp.maximum(m_old, jnp.max(s, axis=-1, keepdims=True))
            alpha = jnp.exp2(m_old - m_new)
            p = jnp.exp2(s - jnp.concatenate([m_new] * (blk // LANES), axis=1))
            l_scr[half] = alpha * l_scr[half] + jnp.sum(p, axis=-1, keepdims=True)
            acc_scr[half] = (jnp.concatenate([alpha] * (DV_C // LANES), axis=1) * acc_scr[half]
                             + _dot(p.astype(BF16), vb))
            m_scr[half] = m_new

    produce(0, sa_scr)

    def pair(t, carry):
        produce(2 * t + 1, sb_scr)
        consume(2 * t, sa_scr, False)
        produce(2 * t + 2, sa_scr)
        consume(2 * t + 1, sb_scr, False)
        return carry

    lax.fori_loop(0, qi // 2, pair, 0)

    @pl.when(qi % 2 == 0)
    def _():
        consume(qi, sa_scr, True)

    @pl.when(qi % 2 == 1)
    def _():
        produce(qi, sb_scr)
        consume(qi - 1, sa_scr, False)
        consume(qi, sb_scr, True)

    o_ref[...] = _diff_out(acc_scr[0], l_scr[0][:, :1], acc_scr[1], l_scr[1][:, :1],
                           lam_ref[0, 0], g_ref[...], out_scale)


def _attn_prompt(q, k, v, lam, g, *, blk, out_scale):
    b, l, _ = q.shape
    kern = functools.partial(_attn_prompt_kernel, blk=blk, out_scale=out_scale)
    kv_spec = pl.BlockSpec((None, l, DV_C), lambda bi, hi, qi: (bi, 0, hi))
    return pl.pallas_call(
        kern,
        grid=(b, H_C, l // blk),
        in_specs=[pl.BlockSpec(memory_space=pltpu.SMEM),
                  pl.BlockSpec((None, blk, DV_C), lambda bi, hi, qi: (bi, qi, hi)),
                  kv_spec, kv_spec,
                  pl.BlockSpec((1, DV_C), lambda bi, hi, qi: (0, 0))],
        out_specs=pl.BlockSpec((None, blk, DV_C), lambda bi, hi, qi: (bi, qi, hi)),
        out_shape=jax.ShapeDtypeStruct((b, l, W_C), BF16),
        scratch_shapes=[pltpu.VMEM((2, blk, DV_C), F32),
                        pltpu.VMEM((2, blk, blk), F32),
                        pltpu.VMEM((2, blk, blk), F32),
                        pltpu.VMEM((2, blk, LANES), F32),
                        pltpu.VMEM((2, blk, LANES), F32)],
        compiler_params=_cparams(3),
        name="attn_prompt",
    )(lam, q, k, v, g)


def _attn_sample_kernel(lam_ref, q_ref, kn_ref, vn_ref, k0_ref, k1_ref, v0_ref, v1_ref, g_ref, o_ref,
                        kp_scr, vp_scr, *, out_scale):
    past = kp_scr.shape[0]
    for dst, halves in ((kp_scr, (k0_ref, k1_ref)), (vp_scr, (v0_ref, v1_ref))):
        for c, c_ref in enumerate(halves):
            rows = c_ref.reshape(past * H_C, LANES)
            for h in range(H_C):
                lo = h * DV_C + c * LANES
                dst[:, lo:lo + LANES] = rows[pl.ds(h, past, stride=H_C), :].astype(BF16)

    lam = lam_ref[0, 0]
    g = g_ref[...]
    for h in range(H_C):
        hd = slice(h * DV_C, (h + 1) * DV_C)
        q = q_ref[:, hd]
        kn = kn_ref[:, hd]
        vn = vn_ref[:, hd]
        vp = vp_scr[:, hd]
        outs = []
        for half in range(2):
            hs = slice(half * DH_C, (half + 1) * DH_C)
            s_p = _dot_nt(q[:, hs], kp_scr[:, h * DV_C + half * DH_C:h * DV_C + (half + 1) * DH_C])
            s_n = _dot_nt(q[:, hs], kn[:, hs])
            m = jnp.maximum(jnp.max(s_p, axis=-1, keepdims=True), jnp.max(s_n, axis=-1, keepdims=True))
            p_p = jnp.exp2(s_p - m)
            p_n = jnp.exp2(s_n - m)
            l = jnp.sum(p_p, axis=-1, keepdims=True) + jnp.sum(p_n, axis=-1, keepdims=True)
            outs += [_dot(p_p.astype(BF16), vp) + _dot(p_n.astype(BF16), vn), l]
        o_ref[:, hd] = _diff_out(outs[0], outs[1], outs[2], outs[3], lam, g, out_scale)


def _attn_sample(q, kn, vn, cache_k, cache_v, layer, lam, g, *, out_scale):
    b, n, _ = q.shape
    past = cache_k.shape[2]
    kern = functools.partial(_attn_sample_kernel, out_scale=out_scale)
    new_spec = pl.BlockSpec((None, n, W_C), lambda bi: (bi, 0, 0))

    def past_spec(c):
        return pl.BlockSpec((None, None, past, H_C, LANES), lambda bi: (layer, bi, 0, 0, c))

    return pl.pallas_call(
        kern,
        grid=(b,),
        in_specs=[pl.BlockSpec(memory_space=pltpu.SMEM),
                  new_spec, new_spec, new_spec, past_spec(0), past_spec(1), past_spec(0), past_spec(1),
                  pl.BlockSpec((1, DV_C), lambda bi: (0, 0))],
        out_specs=new_spec,
        out_shape=jax.ShapeDtypeStruct((b, n, W_C), BF16),
        scratch_shapes=[pltpu.VMEM((past, W_C), BF16), pltpu.VMEM((past, W_C), BF16)],
        compiler_params=_cparams(1),
        name="attn_sample",
    )(lam, q, kn, vn, cache_k, cache_k, cache_v, cache_v, g)


def _outproj_kernel(oa_ref, ob_ref, oc_ref, wa_ref, wb_ref, wc_ref, x_ref, gpost_ref, gpre_ref,
                    x1_ref, hm_ref, *, n_split):
    rows = x_ref.shape[0] // n_split
    for r in range(n_split):
        rs = slice(r * rows, (r + 1) * rows)
        y = (_dot(oa_ref[rs, :], wa_ref[...]) + _dot(ob_ref[rs, :], wb_ref[...])
             + _dot(oc_ref[rs, :], wc_ref[...]))
        x1 = x_ref[rs, :] + y * _rms_scale(y) * gpost_ref[...]
        x1_ref[rs, :] = x1
        hm_ref[rs, :] = (x1 * _rms_scale(x1) * gpre_ref[...]).astype(BF16)


def _outproj(oa, ob, oc, w_out, layer, x, gpost, gpre, *, tm):
    t = x.shape[0]
    row = lambda w: pl.BlockSpec((tm, w), lambda i: (i, 0))
    vec = pl.BlockSpec((1, D_MODEL), lambda i: (0, 0))
    return pl.pallas_call(
        functools.partial(_outproj_kernel, n_split=2),
        grid=(t // tm,),
        in_specs=[row(W_A), row(W_B), row(W_C),
                  pl.BlockSpec((None, W_A, D_MODEL), lambda i: (layer, 0, 0)),
                  pl.BlockSpec((None, W_B, D_MODEL), lambda i: (layer, 1, 0)),
                  pl.BlockSpec((None, W_C, D_MODEL), lambda i: (layer, 1, 0)),
                  row(D_MODEL), vec, vec],
        out_specs=[row(D_MODEL), row(D_MODEL)],
        out_shape=[jax.ShapeDtypeStruct((t, D_MODEL), F32),
                   jax.ShapeDtypeStruct((t, D_MODEL), BF16)],
        compiler_params=_cparams(1),
        name="outproj",
    )(oa, ob, oc, w_out, w_out, w_out, x, gpost, gpre)


def _mlp_kernel(hm_ref, wu_ref, wd_ref, x1_ref, g_ref, o_ref, acc_scr):
    f = pl.program_id(1)

    @pl.when(f == 0)
    def _():
        acc_scr[...] = jnp.zeros_like(acc_scr)

    u = jnp.maximum(_dot(hm_ref[...], wu_ref[...]), 0.0)
    acc_scr[...] += _dot((u * u).astype(BF16), wd_ref[...])

    @pl.when(f == pl.num_programs(1) - 1)
    def _():
        y = acc_scr[...]
        o_ref[...] = x1_ref[...] + y * _rms_scale(y) * g_ref[...]


def _mlp(hm, w_up, w_down, layer, x1, g, *, tm, tf):
    t = hm.shape[0]
    row = lambda dt: pl.BlockSpec((tm, D_MODEL), lambda i, f: (i, 0))
    return pl.pallas_call(
        _mlp_kernel,
        grid=(t // tm, D_FF // tf),
        in_specs=[row(BF16),
                  pl.BlockSpec((None, D_MODEL, tf), lambda i, f: (layer, 0, f)),
                  pl.BlockSpec((None, tf, D_MODEL), lambda i, f: (layer, f, 0)),
                  row(F32),
                  pl.BlockSpec((1, D_MODEL), lambda i, f: (0, 0))],
        out_specs=row(F32),
        out_shape=jax.ShapeDtypeStruct((t, D_MODEL), F32),
        scratch_shapes=[pltpu.VMEM((tm, D_MODEL), F32)],
        compiler_params=_cparams(2),
        name="mlp",
    )(hm, w_up, w_down, x1, g)


def _rotary_tables(pos):
    inv = 1.0 / (RET_THETA_BASE ** jnp.linspace(0.0, 1.0, HALF_B, dtype=F32))
    ang = pos[:, None] * inv[None]
    return jnp.tile(jnp.cos(ang), (1, H_B)), jnp.tile(jnp.sin(ang), (1, H_B))


def _prep_w_ret(w):
    def regroup(lo):
        return [w[..., lo + h * DK_B + half * HALF_B:lo + h * DK_B + (half + 1) * HALF_B]
                for half in range(2) for h in range(H_B)]

    wq = H_B * DK_B
    return jnp.concatenate(regroup(RET_QK_LO) + regroup(RET_QK_LO + wq), axis=-1).astype(BF16)


def _pick(n, pref):
    return pref if n % pref == 0 else n


def _layer(x, batch, s_a, s_b, rot, cache, rows_prev, layer, depth, lb, lam, lam_init, wts):
    w_in, w_out, w_up, w_down, g_pre_mix, g_post_mix, g_pre_mlp, g_post_mlp, g_hgrn, g_diff = wts
    t = x.shape[0]
    l = t // batch
    tm = _pick(t, 512)
    ab, q, kbf, vbf, k_rows, v_rows = _inproj(x, g_pre_mix, w_in, rows_prev, layer, depth,
                                              tm=_pick(t, 256), tn=512)
    ab3 = ab.reshape(batch, l, W_AB)
    oa, sa_new = _hgrn(ab3, s_a, lb, g_hgrn, first_layer=(layer == 0), tc=_pick(l, 512),
                       nb=_pick(batch, 2))
    ob, sb_new = _retention(ab3, s_b, rot[0], rot[1], tc=_pick(l, 256), nb=_pick(batch, 2))
    q3, k3, v3 = (a.reshape(batch, l, W_CQ) for a in (q, kbf, vbf))
    out_scale = 1.0 - lam_init
    if cache is None:
        oc = _attn_prompt(q3, k3, v3, lam, g_diff, blk=512, out_scale=out_scale)
    else:
        oc = _attn_sample(q3, k3, v3, cache[0], cache[1], layer, lam, g_diff, out_scale=out_scale)
    x1, hm = _outproj(oa.reshape(t, W_A), ob.reshape(t, W_B), oc.reshape(t, W_C), w_out, layer, x,
                      g_post_mix, g_pre_mlp, tm=tm)
    x2 = _mlp(hm, w_up, w_down, layer, x1, g_post_mlp, tm=tm, tf=1024)
    return x2, sa_new, sb_new, (k_rows, v_rows)


def kernel(x_prompt, x_sample, state_hgrn, state_ret, cache_k, cache_v, w_in, w_out, w_up, w_down, g_pre_mix, g_post_mix, g_pre_mlp, g_post_mlp, hgrn_lb_logits, g_hgrn_out, lambda_q1, lambda_k1, lambda_q2, lambda_k2, g_diff_out):
    depth = w_in.shape[0]
    b_p, seq_p, _ = x_prompt.shape
    b_s, n_new, _ = x_sample.shape
    past_len = cache_k.shape[2]

    rot_p = _rotary_tables(jnp.arange(seq_p, dtype=F32))
    rot_s = _rotary_tables(past_len + jnp.arange(n_new, dtype=F32))
    lb_soft = jax.nn.softmax(hgrn_lb_logits.astype(F32), axis=0)
    lower_bounds = jnp.cumsum(lb_soft, axis=0) - lb_soft[0]
    cache = (cache_k, cache_v)
    sa0 = jnp.zeros((b_p, H_A, DK_A, DV_A), F32)
    sb0 = jnp.zeros((b_p, H_B, DK_B, DV_B), F32)

    hp = x_prompt.reshape(b_p * seq_p, D_MODEL)
    hs = x_sample.reshape(b_s * n_new, D_MODEL)
    w_in_bf, w_out_bf, w_up_bf, w_down_bf = ((w_in.astype(BF16), _prep_w_ret(w_in)), w_out.astype(BF16),
                                             w_up.astype(BF16), w_down.astype(BF16))
    states_p, states_s = [], []
    rows_p = rows_s = None
    for l in range(depth):
        lam_init = 0.8 - 0.6 * math.exp(-0.3 * l)
        lam = (jnp.exp(jnp.sum(lambda_q1[l] * lambda_k1[l])) - jnp.exp(jnp.sum(lambda_q2[l] * lambda_k2[l]))
               + lam_init).astype(F32).reshape(1, 1)
        vec = lambda a: a[l].reshape(1, -1)
        wts = (w_in_bf, w_out_bf, w_up_bf, w_down_bf,
               vec(g_pre_mix), vec(g_post_mix), vec(g_pre_mlp), vec(g_post_mlp), vec(g_hgrn_out),
               vec(g_diff_out))
        lb = lower_bounds[l].reshape(1, -1)
        hp, sa_p, sb_p, rows_p = _layer(hp, b_p, sa0, sb0, rot_p, None, rows_p, l, depth, lb, lam,
                                        lam_init, wts)
        hs, sa_s, sb_s, rows_s = _layer(hs, b_s, state_hgrn[l], state_ret[l], rot_s, cache, rows_s, l, depth,
                                        lb, lam, lam_init, wts)
        states_p.append((sa_p, sb_p))
        states_s.append((sa_s, sb_s))

    stack = lambda outs, i: jnp.stack([o[i] for o in outs])
    rows = lambda a, b, n: a.reshape(depth, b, n, H_C, DV_C)
    return (hp.reshape(b_p, seq_p, D_MODEL), hs.reshape(b_s, n_new, D_MODEL),
            stack(states_p, 0), stack(states_s, 0), stack(states_p, 1), stack(states_s, 1),
            rows(rows_p[0], b_p, seq_p), rows(rows_s[0], b_s, n_new),
            rows(rows_p[1], b_p, seq_p), rows(rows_s[1], b_s, n_new))
```

```python
import functools
import math

import jax
import jax.numpy as jnp
from jax import lax
from jax.experimental import pallas as pl
from jax.experimental.pallas import tpu as pltpu

F32 = jnp.float32
BF16 = jnp.bfloat16

D_MODEL = 2048
CHUNK = 64
EPS = 1e-6
H_A, DK_A, DV_A = 4, 128, 128
H_B, DK_B, DV_B = 4, 64, 128
RET_THETA_BASE = 10000.0
H_C, DH_C, DV_C = 4, 128, 256
D_FF = 4 * D_MODEL
W_A = H_A * DV_A
W_B = H_B * DV_B
W_C = H_C * DV_C
W_AB = 4 * H_A * DK_A + 2 * H_B * DK_B + 2 * H_B * DV_B
W_CQ = H_C * 2 * DH_C
IN_WIDTH = W_AB + 3 * W_CQ
HALF_B = DK_B // 2
RET_QK_LO = 4 * H_A * DK_A
RET_QK_W = 2 * H_B * DK_B

VMEM_LIMIT_BYTES = 56 * 1024 * 1024
SUBCHUNK = 16
NEG_BIG = -1e30
LOG2_E = 1.4426950408889634
LANES = 128


def _cparams(n_axes):
    return pltpu.CompilerParams(dimension_semantics=("arbitrary",) * n_axes,
                                vmem_limit_bytes=VMEM_LIMIT_BYTES)


def _rms_scale(x):
    return lax.rsqrt(jnp.mean(x * x, axis=-1, keepdims=True) + EPS)


def _sigmoid(x):
    return 1.0 / (1.0 + jnp.exp(-x))


def _dot(a, b):
    return jnp.dot(a, b, preferred_element_type=F32)


def _dot_nt(a, b):
    return lax.dot_general(a, b, (((1,), (1,)), ((), ())), preferred_element_type=F32)


def _dot_tn(a, b):
    return lax.dot_general(a, b, (((0,), (0,)), ((), ())), preferred_element_type=F32)


def _inproj_kernel(*refs, q_scale, aliased, tn):
    x_ref, g_ref, w_ref, wret_ref = refs[:4]
    ab_ref, q_ref, kbf_ref, vbf_ref, krow_ref, vrow_ref, h_scr = refs[6 if aliased else 4:]
    x = x_ref[...]
    h_scr[...] = (x * _rms_scale(x) * g_ref[...]).astype(BF16)
    heads_per_group = tn // DV_C

    def cols(lo):
        w = wret_ref[...] if lo == RET_QK_LO else w_ref[:, lo:lo + tn]
        return _dot(h_scr[...], w)

    for c in range(W_AB // tn):
        ab_ref[:, c * tn:(c + 1) * tn] = cols(c * tn)
    for c in range(W_CQ // tn):
        q_ref[:, c * tn:(c + 1) * tn] = (cols(W_AB + c * tn) * q_scale).astype(BF16)
    for group, (bf_ref, row_ref) in enumerate(((kbf_ref, krow_ref), (vbf_ref, vrow_ref))):
        for c in range(W_CQ // tn):
            acc = cols(W_AB + (1 + group) * W_CQ + c * tn)
            bf_ref[:, c * tn:(c + 1) * tn] = acc.astype(BF16)
            for hh in range(heads_per_group):
                row_ref[:, c * heads_per_group + hh, :] = acc[:, hh * DV_C:(hh + 1) * DV_C]


def _inproj(x, g, w, rows_prev, layer, depth, *, tm, tn):
    t = x.shape[0]
    w_all, w_ret = w
    assert tn == RET_QK_W and RET_QK_LO % tn == 0
    aliased = rows_prev is not None
    kern = functools.partial(_inproj_kernel, q_scale=DH_C ** -0.5 * LOG2_E, aliased=aliased, tn=tn)
    row_spec = pl.BlockSpec((None, tm, H_C, DV_C), lambda i: (layer, i, 0, 0))
    row_shape = jax.ShapeDtypeStruct((depth, t, H_C, DV_C), F32)
    tok = lambda width: pl.BlockSpec((tm, width), lambda i: (i, 0))
    in_specs = [tok(D_MODEL),
                pl.BlockSpec((1, D_MODEL), lambda i: (0, 0)),
                pl.BlockSpec((None, D_MODEL, IN_WIDTH), lambda i: (layer, 0, 0),
                             pipeline_mode=pl.Buffered(1)),
                pl.BlockSpec((None, D_MODEL, RET_QK_W), lambda i: (layer, 0, 0),
                             pipeline_mode=pl.Buffered(1))]
    args = [x, g, w_all, w_ret]
    if aliased:
        in_specs += [pl.BlockSpec(memory_space=pl.ANY)] * 2
        args += list(rows_prev)
    return pl.pallas_call(
        kern,
        grid=(t // tm,),
        in_specs=in_specs,
        out_specs=[tok(W_AB), tok(W_CQ), tok(W_CQ), tok(W_CQ), row_spec, row_spec],
        out_shape=[jax.ShapeDtypeStruct((t, W_AB), F32),
                   jax.ShapeDtypeStruct((t, W_CQ), BF16),
                   jax.ShapeDtypeStruct((t, W_CQ), BF16),
                   jax.ShapeDtypeStruct((t, W_CQ), BF16),
                   row_shape, row_shape],
        input_output_aliases={4: 4, 5: 5} if aliased else {},
        scratch_shapes=[pltpu.VMEM((tm, D_MODEL), BF16)],
        compiler_params=_cparams(1),
        name="inproj",
    )(*args)


def _hgrn_kernel(lb_ref, gout_ref, s0_ref, q_ref, f_ref, i_ref, g_ref, o_ref, sfin_ref, st_scr, row_scr,
                 *, first_layer, n_chunks, nb):
    t = pl.program_id(1)

    @pl.when(t == 0)
    def _():
        for bb in range(nb):
            for h in range(H_A):
                st_scr[bb, h] = s0_ref[bb, h].T

    rows = lax.broadcasted_iota(jnp.int32, (CHUNK, CHUNK), 0)
    cols = lax.broadcasted_iota(jnp.int32, (CHUNK, CHUNK), 1)
    tri = (cols <= rows).astype(BF16)
    sub_rows = lax.broadcasted_iota(jnp.int32, (SUBCHUNK, 1), 0)
    gout = gout_ref[...]
    lb = lb_ref[...]
    n_sub = CHUNK // SUBCHUNK

    def chunk_rows(bb, r0):
        zq = q_ref[bb, pl.ds(r0, CHUNK), :]
        zf = f_ref[bb, pl.ds(r0, CHUNK), :]
        vi = i_ref[bb, pl.ds(r0, CHUNK), :]
        zg = g_ref[bb, pl.ds(r0, CHUNK), :]

        e = jnp.exp(-jnp.abs(zf))
        r = 1.0 / (1.0 + e)
        pos = zf >= 0.0
        if first_layer:
            logf = jnp.minimum(zf, 0.0) - jnp.log(1.0 + e)
        else:
            logf = jnp.log(lb + (1.0 - lb) * jnp.where(pos, r, e * r))
        kk = (1.0 - lb) * jnp.where(pos, e * r, r)
        qs = zq * _sigmoid(zq) * (DK_A ** -0.5)
        gate = zg * _sigmoid(zg)

        g_hi = logf.astype(BF16)
        r1 = logf - g_hi.astype(F32)
        g_mid = r1.astype(BF16)
        g_lo = (r1 - g_mid.astype(F32)).astype(BF16)
        bcum = (_dot(tri, g_hi) + _dot(tri, g_mid) + _dot(tri, g_lo)) * LOG2_E
        row_scr[bb, 0] = bcum
        row_scr[bb, 1] = kk
        row_scr[bb, 2] = vi

        for h in range(H_A):
            sl = slice(h * DK_A, (h + 1) * DK_A)
            b = bcum[:, sl]
            q_h = qs[:, sl]
            k_h = kk[:, sl]
            v_bf = vi[:, sl].astype(BF16)
            st = st_scr[bb, h]
            btot = b[CHUNK - 1:CHUNK, :]

            o_inter = _dot_nt((q_h * jnp.exp2(b)).astype(BF16), st.astype(BF16))
            kdec = (k_h * jnp.exp2(btot - b)).astype(BF16)
            st_scr[bb, h] = st * jnp.exp2(btot) + _dot_tn(v_bf, kdec)

            for i in range(n_sub):
                lo = i * SUBCHUNK
                b_i = b[lo:lo + SUBCHUNK, :]
                q_i = q_h[lo:lo + SUBCHUNK, :]
                o_i = o_inter[lo:lo + SUBCHUNK, :]
                if i > 0:
                    ref_row = b[lo - 1:lo, :]
                    qd = (q_i * jnp.exp2(b_i - ref_row)).astype(BF16)
                    kd = (k_h[:lo, :] * jnp.exp2(ref_row - b[:lo, :])).astype(BF16)
                    a = _dot_nt(qd, kd)
                    o_i = o_i + _dot(a.astype(BF16), v_bf[:lo, :])
                for s in range(SUBCHUNK):
                    sg = lo + s
                    b_s = row_scr[bb, 0, sg:sg + 1, sl]
                    k_s = row_scr[bb, 1, sg:sg + 1, sl]
                    v_s = row_scr[bb, 2, sg:sg + 1, sl]
                    d = jnp.exp2(b_i - b_s)
                    w = jnp.sum(q_i * d * k_s, axis=-1, keepdims=True)
                    w = jnp.where(sub_rows >= s, w, 0.0)
                    o_i = o_i + w * v_s
                y = o_i * _rms_scale(o_i) * gout * gate[lo:lo + SUBCHUNK, sl]
                o_ref[bb, pl.ds(r0 + lo, SUBCHUNK), sl] = y.astype(BF16)

    def chunk_body(c, carry):
        r0 = pl.multiple_of(c * CHUNK, CHUNK)
        for bb in range(nb):
            chunk_rows(bb, r0)
        return carry

    lax.fori_loop(0, n_chunks, chunk_body, 0)

    @pl.when(t == pl.num_programs(1) - 1)
    def _():
        for bb in range(nb):
            for h in range(H_A):
                sfin_ref[bb, h] = st_scr[bb, h].T


def _hgrn(ab, s0, lb, gout, *, first_layer, tc, nb):
    b, l, _ = ab.shape
    n_chunks = tc // CHUNK
    kern = functools.partial(_hgrn_kernel, first_layer=first_layer, n_chunks=n_chunks, nb=nb)
    blk = H_A * DK_A

    def col(k):
        return pl.BlockSpec((nb, tc, blk), lambda bi, ti: (bi, ti, k))

    state_spec = pl.BlockSpec((nb, H_A, DK_A, DV_A), lambda bi, ti: (bi, 0, 0, 0))
    return pl.pallas_call(
        kern,
        grid=(b // nb, l // tc),
        in_specs=[pl.BlockSpec((1, blk), lambda bi, ti: (0, 0)),
                  pl.BlockSpec((1, DV_A), lambda bi, ti: (0, 0)),
                  state_spec, col(0), col(1), col(2), col(3)],
        out_specs=[pl.BlockSpec((nb, tc, W_A), lambda bi, ti: (bi, ti, 0)), state_spec],
        out_shape=[jax.ShapeDtypeStruct((b, l, W_A), BF16),
                   jax.ShapeDtypeStruct((b, H_A, DK_A, DV_A), F32)],
        scratch_shapes=[pltpu.VMEM((nb, H_A, DV_A, DK_A), F32),
                        pltpu.VMEM((nb, 3, CHUNK, blk), F32)],
        compiler_params=_cparams(2),
        name="hgrn",
    )(lb, gout, s0, ab, ab, ab, ab)


_AB_RET_Q = (4 * H_A * DK_A) // (H_B * DK_B)
_AB_RET_V = (4 * H_A * DK_A + 2 * H_B * DK_B) // W_B
_LOG_GAMMA = tuple(math.log1p(-(2.0 ** (-5.0 - h))) for h in range(H_B))


def _ret_kernel(cos_ref, sin_ref, s0_ref, q_ref, k_ref, v_ref, g_ref, o_ref, sfin_ref, s_scr, *, tc, nb):
    t = pl.program_id(1)
    wq = H_B * DK_B
    wh = H_B * HALF_B

    @pl.when(t == 0)
    def _():
        s_scr[...] = jnp.zeros_like(s_scr)
        for bb in range(nb):
            for h in range(H_B):
                s_scr[bb, h, h * HALF_B:(h + 1) * HALF_B, :] = s0_ref[bb, h, :HALF_B, :]
                s_scr[bb, h, wh + h * HALF_B:wh + (h + 1) * HALF_B, :] = s0_ref[bb, h, HALF_B:, :]

    c = cos_ref[...]
    s = sin_ref[...]

    def rot(x):
        x1, x2 = x[:, :wh], x[:, wh:]
        return jnp.concatenate([x1 * c - x2 * s, x1 * s + x2 * c], axis=1)

    lane_head = (lax.broadcasted_iota(jnp.int32, (1, wq), 1) % wh) // HALF_B
    rel = (lax.broadcasted_iota(jnp.int32, (tc, tc), 0)
           - lax.broadcasted_iota(jnp.int32, (tc, tc), 1))
    relf = jnp.maximum(rel, 0).astype(F32)
    tpos = lax.broadcasted_iota(jnp.int32, (tc, 1), 0).astype(F32)
    dmats = [jnp.where(rel >= 0, jnp.exp(relf * lg), 0.0) for lg in _LOG_GAMMA]
    inner = [jnp.exp((tpos + 1.0) * lg) for lg in _LOG_GAMMA]
    tail = [jnp.exp((tc - 1.0 - tpos) * lg) for lg in _LOG_GAMMA]

    for bb in range(nb):
        qr = rot(q_ref[bb]).astype(BF16)
        kr = rot(k_ref[bb]) * (DK_B ** -0.5)
        v = v_ref[bb]
        gate_in = g_ref[bb]
        for h in range(H_B):
            sl = slice(h * DV_B, (h + 1) * DV_B)
            kh = jnp.where(lane_head == h, kr, 0.0)
            v_bf = v[:, sl].astype(BF16)
            sc = _dot_nt(qr, kh.astype(BF16)) * dmats[h]
            st = s_scr[bb, h]
            o = _dot(sc.astype(BF16), v_bf) + _dot(qr, st.astype(BF16)) * inner[h]
            ktail = (kh * tail[h]).astype(BF16)
            s_scr[bb, h] = math.exp(tc * _LOG_GAMMA[h]) * st + _dot_tn(ktail, v_bf)
            zg = gate_in[:, sl]
            o_ref[bb, :, sl] = (o * _rms_scale(o) * (zg * _sigmoid(zg))).astype(BF16)

    @pl.when(t == pl.num_programs(1) - 1)
    def _():
        for bb in range(nb):
            for h in range(H_B):
                sfin_ref[bb, h, :HALF_B, :] = s_scr[bb, h, h * HALF_B:(h + 1) * HALF_B, :]
                sfin_ref[bb, h, HALF_B:, :] = s_scr[bb, h, wh + h * HALF_B:wh + (h + 1) * HALF_B, :]


def _retention(ab, s0, cos, sin, *, tc, nb):
    b, l, _ = ab.shape
    wq = H_B * DK_B
    wh = H_B * HALF_B
    kern = functools.partial(_ret_kernel, tc=tc, nb=nb)
    state_spec = pl.BlockSpec((nb, H_B, DK_B, DV_B), lambda bi, ti: (bi, 0, 0, 0))
    return pl.pallas_call(
        kern,
        grid=(b // nb, l // tc),
        in_specs=[pl.BlockSpec((tc, wh), lambda bi, ti: (ti, 0)),
                  pl.BlockSpec((tc, wh), lambda bi, ti: (ti, 0)),
                  state_spec,
                  pl.BlockSpec((nb, tc, wq), lambda bi, ti: (bi, ti, _AB_RET_Q)),
                  pl.BlockSpec((nb, tc, wq), lambda bi, ti: (bi, ti, _AB_RET_Q + 1)),
                  pl.BlockSpec((nb, tc, W_B), lambda bi, ti: (bi, ti, _AB_RET_V)),
                  pl.BlockSpec((nb, tc, W_B), lambda bi, ti: (bi, ti, _AB_RET_V + 1))],
        out_specs=[pl.BlockSpec((nb, tc, W_B), lambda bi, ti: (bi, ti, 0)), state_spec],
        out_shape=[jax.ShapeDtypeStruct((b, l, W_B), BF16),
                   jax.ShapeDtypeStruct((b, H_B, DK_B, DV_B), F32)],
        scratch_shapes=[pltpu.VMEM((nb, H_B, wq, DV_B), F32)],
        compiler_params=_cparams(2),
        name="retention",
    )(cos, sin, s0, ab, ab, ab, ab)


def _diff_out(o1, l1, o2, l2, lam, g, out_scale):
    o = o1 / l1 - lam * (o2 / l2)
    return (o * _rms_scale(o) * g * out_scale).astype(BF16)


def _attn_prompt_kernel(lam_ref, q_ref, k_ref, v_ref, g_ref, o_ref, acc_scr, sa_scr, sb_scr, m_scr, l_scr,
                        *, blk, out_scale):
    qi = pl.program_id(2)
    q = q_ref[...]
    qh = (q[:, :DH_C], q[:, DH_C:])
    rq = lax.broadcasted_iota(jnp.int32, (blk, blk), 0) // CHUNK
    ck = lax.broadcasted_iota(jnp.int32, (blk, blk), 1) // CHUNK
    visible = ck <= rq
    acc_scr[...] = jnp.zeros_like(acc_scr)
    m_scr[...] = jnp.full(m_scr.shape, NEG_BIG, F32)
    l_scr[...] = jnp.zeros_like(l_scr)

    def produce(j, buf):
        off = pl.multiple_of(j * blk, blk)
        kb = k_ref[pl.ds(off, blk), :]
        for half in range(2):
            buf[half] = _dot_nt(qh[half], kb[:, half * DH_C:(half + 1) * DH_C])

    def consume(j, buf, masked):
        off = pl.multiple_of(j * blk, blk)
        vb = v_ref[pl.ds(off, blk), :]
        for half in range(2):
            m_old = m_scr[half]
            s = buf[half]
            if masked:
                s = jnp.where(visible, s, -jnp.inf)
            m_new = jnp.maximum(m_old, jnp.max(s, axis=-1, keepdims=True))
            alpha = jnp.exp2(m_old - m_new)
            p = jnp.exp2(s - jnp.concatenate([m_new] * (blk // LANES), axis=1))
            l_scr[half] = alpha * l_scr[half] + jnp.sum(p, axis=-1, keepdims=True)
            acc_scr[half] = (jnp.concatenate([alpha] * (DV_C // LANES), axis=1) * acc_scr[half]
                             + _dot(p.astype(BF16), vb))
            m_scr[half] = m_new

    produce(0, sa_scr)

    def pair(t, carry):
        produce(2 * t + 1, sb_scr)
        consume(2 * t, sa_scr, False)
        produce(2 * t + 2, sa_scr)
        consume(2 * t + 1, sb_scr, False)
        return carry

    lax.fori_loop(0, qi // 2, pair, 0)

    @pl.when(qi % 2 == 0)
    def _():
        consume(qi, sa_scr, True)

    @pl.when(qi % 2 == 1)
    def _():
        produce(qi, sb_scr)
        consume(qi - 1, sa_scr, False)
        consume(qi, sb_scr, True)

    o_ref[...] = _diff_out(acc_scr[0], l_scr[0][:, :1], acc_scr[1], l_scr[1][:, :1],
                           lam_ref[0, 0], g_ref[...], out_scale)


def _attn_prompt(q, k, v, lam, g, *, blk, out_scale):
    b, l, _ = q.shape
    kern = functools.partial(_attn_prompt_kernel, blk=blk, out_scale=out_scale)
    kv_spec = pl.BlockSpec((None, l, DV_C), lambda bi, hi, qi: (bi, 0, hi))
    return pl.pallas_call(
        kern,
        grid=(b, H_C, l // blk),
        in_specs=[pl.BlockSpec(memory_space=pltpu.SMEM),
                  pl.BlockSpec((None, blk, DV_C), lambda bi, hi, qi: (bi, qi, hi)),
                  kv_spec, kv_spec,
                  pl.BlockSpec((1, DV_C), lambda bi, hi, qi: (0, 0))],
        out_specs=pl.BlockSpec((None, blk, DV_C), lambda bi, hi, qi: (bi, qi, hi)),
        out_shape=jax.ShapeDtypeStruct((b, l, W_C), BF16),
        scratch_shapes=[pltpu.VMEM((2, blk, DV_C), F32),
                        pltpu.VMEM((2, blk, blk), F32),
                        pltpu.VMEM((2, blk, blk), F32),
                        pltpu.VMEM((2, blk, LANES), F32),
                        pltpu.VMEM((2, blk, LANES), F32)],
        compiler_params=_cparams(3),
        name="attn_prompt",
    )(lam, q, k, v, g)


def _attn_sample_kernel(lam_ref, q_ref, kn_ref, vn_ref, k0_ref, k1_ref, v0_ref, v1_ref, g_ref, o_ref,
                        kp_scr, vp_scr, *, out_scale):
    past = kp_scr.shape[0]
    for dst, halves in ((kp_scr, (k0_ref, k1_ref)), (vp_scr, (v0_ref, v1_ref))):
        for c, c_ref in enumerate(halves):
            rows = c_ref.reshape(past * H_C, LANES)
            for h in range(H_C):
                lo = h * DV_C + c * LANES
                dst[:, lo:lo + LANES] = rows[pl.ds(h, past, stride=H_C), :].astype(BF16)

    lam = lam_ref[0, 0]
    g = g_ref[...]
    for h in range(H_C):
        hd = slice(h * DV_C, (h + 1) * DV_C)
        q = q_ref[:, hd]
        kn = kn_ref[:, hd]
        vn = vn_ref[:, hd]
        vp = vp_scr[:, hd]
        outs = []
        for half in range(2):
            hs = slice(half * DH_C, (half + 1) * DH_C)
            s_p = _dot_nt(q[:, hs], kp_scr[:, h * DV_C + half * DH_C:h * DV_C + (half + 1) * DH_C])
            s_n = _dot_nt(q[:, hs], kn[:, hs])
            m = jnp.maximum(jnp.max(s_p, axis=-1, keepdims=True), jnp.max(s_n, axis=-1, keepdims=True))
            p_p = jnp.exp2(s_p - m)
            p_n = jnp.exp2(s_n - m)
            l = jnp.sum(p_p, axis=-1, keepdims=True) + jnp.sum(p_n, axis=-1, keepdims=True)
            outs += [_dot(p_p.astype(BF16), vp) + _dot(p_n.astype(BF16), vn), l]
        o_ref[:, hd] = _diff_out(outs[0], outs[1], outs[2], outs[3], lam, g, out_scale)


def _attn_sample(q, kn, vn, cache_k, cache_v, layer, lam, g, *, out_scale):
    b, n, _ = q.shape
    past = cache_k.shape[2]
    kern = functools.partial(_attn_sample_kernel, out_scale=out_scale)
    new_spec = pl.BlockSpec((None, n, W_C), lambda bi: (bi, 0, 0))

    def past_spec(c):
        return pl.BlockSpec((None, None, past, H_C, LANES), lambda bi: (layer, bi, 0, 0, c))

    return pl.pallas_call(
        kern,
        grid=(b,),
        in_specs=[pl.BlockSpec(memory_space=pltpu.SMEM),
                  new_spec, new_spec, new_spec, past_spec(0), past_spec(1), past_spec(0), past_spec(1),
                  pl.BlockSpec((1, DV_C), lambda bi: (0, 0))],
        out_specs=new_spec,
        out_shape=jax.ShapeDtypeStruct((b, n, W_C), BF16),
        scratch_shapes=[pltpu.VMEM((past, W_C), BF16), pltpu.VMEM((past, W_C), BF16)],
        compiler_params=_cparams(1),
        name="attn_sample",
    )(lam, q, kn, vn, cache_k, cache_k, cache_v, cache_v, g)


def _outproj_kernel(oa_ref, ob_ref, oc_ref, wa_ref, wb_ref, wc_ref, x_ref, gpost_ref, gpre_ref,
                    x1_ref, hm_ref, *, n_split):
    rows = x_ref.shape[0] // n_split
    for r in range(n_split):
        rs = slice(r * rows, (r + 1) * rows)
        y = (_dot(oa_ref[rs, :], wa_ref[...]) + _dot(ob_ref[rs, :], wb_ref[...])
             + _dot(oc_ref[rs, :], wc_ref[...]))
        x1 = x_ref[rs, :] + y * _rms_scale(y) * gpost_ref[...]
        x1_ref[rs, :] = x1
        hm_ref[rs, :] = (x1 * _rms_scale(x1) * gpre_ref[...]).astype(BF16)


def _outproj(oa, ob, oc, w_out, layer, x, gpost, gpre, *, tm):
    t = x.shape[0]
    row = lambda w: pl.BlockSpec((tm, w), lambda i: (i, 0))
    vec = pl.BlockSpec((1, D_MODEL), lambda i: (0, 0))
    return pl.pallas_call(
        functools.partial(_outproj_kernel, n_split=2),
        grid=(t // tm,),
        in_specs=[row(W_A), row(W_B), row(W_C),
                  pl.BlockSpec((None, W_A, D_MODEL), lambda i: (layer, 0, 0)),
                  pl.BlockSpec((None, W_B, D_MODEL), lambda i: (layer, 1, 0)),
                  pl.BlockSpec((None, W_C, D_MODEL), lambda i: (layer, 1, 0)),
                  row(D_MODEL), vec, vec],
        out_specs=[row(D_MODEL), row(D_MODEL)],
        out_shape=[jax.ShapeDtypeStruct((t, D_MODEL), F32),
                   jax.ShapeDtypeStruct((t, D_MODEL), BF16)],
        compiler_params=_cparams(1),
        name="outproj",
    )(oa, ob, oc, w_out, w_out, w_out, x, gpost, gpre)


def _mlp_kernel(hm_ref, wu_hbm, wd_hbm, x1_ref, g_ref, o_ref, wu_buf, wd_buf, sem, acc_scr,
                *, layer, tf, n_f):
    i = pl.program_id(0)
    n_i = pl.num_programs(0)

    def block_copies(f, slot):
        return (pltpu.make_async_copy(wu_hbm.at[layer, :, pl.ds(f * tf, tf)], wu_buf.at[slot], sem.at[0, slot]),
                pltpu.make_async_copy(wd_hbm.at[layer, pl.ds(f * tf, tf), :], wd_buf.at[slot], sem.at[1, slot]))

    def start(f, slot):
        for cp in block_copies(f, slot):
            cp.start()

    @pl.when(i == 0)
    def _():
        start(0, 0)

    acc_scr[...] = jnp.zeros_like(acc_scr)

    def body(f, carry):
        slot = f % 2
        nxt = 1 - slot

        @pl.when(f + 1 < n_f)
        def _():
            start(f + 1, nxt)

        @pl.when((f + 1 == n_f) & (i + 1 < n_i))
        def _():
            start(0, nxt)

        for cp in block_copies(f, slot):
            cp.wait()
        u = jnp.maximum(_dot(hm_ref[...], wu_buf[slot]), 0.0)
        acc_scr[...] += _dot((u * u).astype(BF16), wd_buf[slot])
        return carry

    lax.fori_loop(0, n_f, body, 0)
    y = acc_scr[...]
    o_ref[...] = x1_ref[...] + y * _rms_scale(y) * g_ref[...]


def _mlp(hm, w_up, w_down, layer, x1, g, *, tm, tf):
    t = hm.shape[0]
    n_f = D_FF // tf
    assert n_f % 2 == 0
    row = pl.BlockSpec((tm, D_MODEL), lambda i: (i, 0))
    return pl.pallas_call(
        functools.partial(_mlp_kernel, layer=layer, tf=tf, n_f=n_f),
        grid=(t // tm,),
        in_specs=[row,
                  pl.BlockSpec(memory_space=pl.ANY),
                  pl.BlockSpec(memory_space=pl.ANY),
                  row,
                  pl.BlockSpec((1, D_MODEL), lambda i: (0, 0))],
        out_specs=row,
        out_shape=jax.ShapeDtypeStruct((t, D_MODEL), F32),
        scratch_shapes=[pltpu.VMEM((2, D_MODEL, tf), BF16),
                        pltpu.VMEM((2, tf, D_MODEL), BF16),
                        pltpu.SemaphoreType.DMA((2, 2)),
                        pltpu.VMEM((tm, D_MODEL), F32)],
        compiler_params=_cparams(1),
        name="mlp",
    )(hm, w_up, w_down, x1, g)


def _rotary_tables(pos):
    inv = 1.0 / (RET_THETA_BASE ** jnp.linspace(0.0, 1.0, HALF_B, dtype=F32))
    ang = pos[:, None] * inv[None]
    return jnp.tile(jnp.cos(ang), (1, H_B)), jnp.tile(jnp.sin(ang), (1, H_B))


def _prep_w_ret(w):
    def regroup(lo):
        return [w[..., lo + h * DK_B + half * HALF_B:lo + h * DK_B + (half + 1) * HALF_B]
                for half in range(2) for h in range(H_B)]

    wq = H_B * DK_B
    return jnp.concatenate(regroup(RET_QK_LO) + regroup(RET_QK_LO + wq), axis=-1).astype(BF16)


def _pick(n, pref):
    return pref if n % pref == 0 else n


def _layer(x, batch, s_a, s_b, rot, cache, rows_prev, layer, depth, lb, lam, lam_init, wts):
    w_in, w_out, w_up, w_down, g_pre_mix, g_post_mix, g_pre_mlp, g_post_mlp, g_hgrn, g_diff = wts
    t = x.shape[0]
    l = t // batch
    tm = _pick(t, 512)
    ab, q, kbf, vbf, k_rows, v_rows = _inproj(x, g_pre_mix, w_in, rows_prev, layer, depth,
                                              tm=_pick(t, 256), tn=512)
    ab3 = ab.reshape(batch, l, W_AB)
    oa, sa_new = _hgrn(ab3, s_a, lb, g_hgrn, first_layer=(layer == 0), tc=_pick(l, 512),
                       nb=_pick(batch, 2))
    ob, sb_new = _retention(ab3, s_b, rot[0], rot[1], tc=_pick(l, 256), nb=_pick(batch, 2))
    q3, k3, v3 = (a.reshape(batch, l, W_CQ) for a in (q, kbf, vbf))
    out_scale = 1.0 - lam_init
    if cache is None:
        oc = _attn_prompt(q3, k3, v3, lam, g_diff, blk=512, out_scale=out_scale)
    else:
        oc = _attn_sample(q3, k3, v3, cache[0], cache[1], layer, lam, g_diff, out_scale=out_scale)
    x1, hm = _outproj(oa.reshape(t, W_A), ob.reshape(t, W_B), oc.reshape(t, W_C), w_out, layer, x,
                      g_post_mix, g_pre_mlp, tm=tm)
    x2 = _mlp(hm, w_up, w_down, layer, x1, g_post_mlp, tm=tm, tf=1024)
    return x2, sa_new, sb_new, (k_rows, v_rows)


def kernel(x_prompt, x_sample, state_hgrn, state_ret, cache_k, cache_v, w_in, w_out, w_up, w_down, g_pre_mix, g_post_mix, g_pre_mlp, g_post_mlp, hgrn_lb_logits, g_hgrn_out, lambda_q1, lambda_k1, lambda_q2, lambda_k2, g_diff_out):
    depth = w_in.shape[0]
    b_p, seq_p, _ = x_prompt.shape
    b_s, n_new, _ = x_sample.shape
    past_len = cache_k.shape[2]

    rot_p = _rotary_tables(jnp.arange(seq_p, dtype=F32))
    rot_s = _rotary_tables(past_len + jnp.arange(n_new, dtype=F32))
    lb_soft = jax.nn.softmax(hgrn_lb_logits.astype(F32), axis=0)
    lower_bounds = jnp.cumsum(lb_soft, axis=0) - lb_soft[0]
    cache = (cache_k, cache_v)
    sa0 = jnp.zeros((b_p, H_A, DK_A, DV_A), F32)
    sb0 = jnp.zeros((b_p, H_B, DK_B, DV_B), F32)

    hp = x_prompt.reshape(b_p * seq_p, D_MODEL)
    hs = x_sample.reshape(b_s * n_new, D_MODEL)
    w_in_bf, w_out_bf, w_up_bf, w_down_bf = ((w_in.astype(BF16), _prep_w_ret(w_in)), w_out.astype(BF16),
                                             w_up.astype(BF16), w_down.astype(BF16))
    states_p, states_s = [], []
    rows_p = rows_s = None
    for l in range(depth):
        lam_init = 0.8 - 0.6 * math.exp(-0.3 * l)
        lam = (jnp.exp(jnp.sum(lambda_q1[l] * lambda_k1[l])) - jnp.exp(jnp.sum(lambda_q2[l] * lambda_k2[l]))
               + lam_init).astype(F32).reshape(1, 1)
        vec = lambda a: a[l].reshape(1, -1)
        wts = (w_in_bf, w_out_bf, w_up_bf, w_down_bf,
               vec(g_pre_mix), vec(g_post_mix), vec(g_pre_mlp), vec(g_post_mlp), vec(g_hgrn_out),
               vec(g_diff_out))
        lb = lower_bounds[l].reshape(1, -1)
        hp, sa_p, sb_p, rows_p = _layer(hp, b_p, sa0, sb0, rot_p, None, rows_p, l, depth, lb, lam,
                                        lam_init, wts)
        hs, sa_s, sb_s, rows_s = _layer(hs, b_s, state_hgrn[l], state_ret[l], rot_s, cache, rows_s, l, depth,
                                        lb, lam, lam_init, wts)
        states_p.append((sa_p, sb_p))
        states_s.append((sa_s, sb_s))

    stack = lambda outs, i: jnp.stack([o[i] for o in outs])
    rows = lambda a, b, n: a.reshape(depth, b, n, H_C, DV_C)
    return (hp.reshape(b_p, seq_p, D_MODEL), hs.reshape(b_s, n_new, D_MODEL),
            stack(states_p, 0), stack(states_s, 0), stack(states_p, 1), stack(states_s, 1),
            rows(rows_p[0], b_p, seq_p), rows(rows_s[0], b_s, n_new),
            rows(rows_p[1], b_p, seq_p), rows(rows_s[1], b_s, n_new))
```

```python
import functools
import math

import jax
import jax.numpy as jnp
from jax import lax
from jax.experimental import pallas as pl
from jax.experimental.pallas import tpu as pltpu

F32 = jnp.float32
BF16 = jnp.bfloat16

D_MODEL = 2048
CHUNK = 64
EPS = 1e-6
H_A, DK_A, DV_A = 4, 128, 128
H_B, DK_B, DV_B = 4, 64, 128
RET_THETA_BASE = 10000.0
H_C, DH_C, DV_C = 4, 128, 256
D_FF = 4 * D_MODEL
W_A = H_A * DV_A
W_B = H_B * DV_B
W_C = H_C * DV_C
W_AB = 4 * H_A * DK_A + 2 * H_B * DK_B + 2 * H_B * DV_B
W_CQ = H_C * 2 * DH_C
IN_WIDTH = W_AB + 3 * W_CQ
HALF_B = DK_B // 2
RET_QK_LO = 4 * H_A * DK_A
RET_QK_W = 2 * H_B * DK_B

VMEM_LIMIT_BYTES = 56 * 1024 * 1024
SUBCHUNK = 16
NEG_BIG = -1e30
LOG2_E = 1.4426950408889634
LANES = 128


def _cparams(n_axes):
    return pltpu.CompilerParams(dimension_semantics=("arbitrary",) * n_axes,
                                vmem_limit_bytes=VMEM_LIMIT_BYTES)


def _rms_scale(x):
    return lax.rsqrt(jnp.mean(x * x, axis=-1, keepdims=True) + EPS)


def _sigmoid(x):
    return 1.0 / (1.0 + jnp.exp(-x))


def _dot(a, b):
    return jnp.dot(a, b, preferred_element_type=F32)


def _dot_nt(a, b):
    return lax.dot_general(a, b, (((1,), (1,)), ((), ())), preferred_element_type=F32)


def _dot_tn(a, b):
    return lax.dot_general(a, b, (((0,), (0,)), ((), ())), preferred_element_type=F32)


def _inproj_kernel(*refs, q_scale, aliased, tn):
    x_ref, g_ref, w_ref, wret_ref = refs[:4]
    ab_ref, q_ref, kbf_ref, vbf_ref, krow_ref, vrow_ref, h_scr = refs[6 if aliased else 4:]
    x = x_ref[...]
    h_scr[...] = (x * _rms_scale(x) * g_ref[...]).astype(BF16)
    heads_per_group = tn // DV_C

    def cols(lo):
        w = wret_ref[...] if lo == RET_QK_LO else w_ref[:, lo:lo + tn]
        return _dot(h_scr[...], w)

    for c in range(W_AB // tn):
        ab_ref[:, c * tn:(c + 1) * tn] = cols(c * tn)
    for c in range(W_CQ // tn):
        q_ref[:, c * tn:(c + 1) * tn] = (cols(W_AB + c * tn) * q_scale).astype(BF16)
    for group, (bf_ref, row_ref) in enumerate(((kbf_ref, krow_ref), (vbf_ref, vrow_ref))):
        for c in range(W_CQ // tn):
            acc = cols(W_AB + (1 + group) * W_CQ + c * tn)
            bf_ref[:, c * tn:(c + 1) * tn] = acc.astype(BF16)
            for hh in range(heads_per_group):
                row_ref[:, c * heads_per_group + hh, :] = acc[:, hh * DV_C:(hh + 1) * DV_C]


def _inproj(x, g, w, rows_prev, layer, depth, *, tm, tn):
    t = x.shape[0]
    w_all, w_ret = w
    assert tn == RET_QK_W and RET_QK_LO % tn == 0
    aliased = rows_prev is not None
    kern = functools.partial(_inproj_kernel, q_scale=DH_C ** -0.5 * LOG2_E, aliased=aliased, tn=tn)
    row_spec = pl.BlockSpec((None, tm, H_C, DV_C), lambda i: (layer, i, 0, 0))
    row_shape = jax.ShapeDtypeStruct((depth, t, H_C, DV_C), F32)
    tok = lambda width: pl.BlockSpec((tm, width), lambda i: (i, 0))
    in_specs = [tok(D_MODEL),
                pl.BlockSpec((1, D_MODEL), lambda i: (0, 0)),
                pl.BlockSpec((None, D_MODEL, IN_WIDTH), lambda i: (layer, 0, 0),
                             pipeline_mode=pl.Buffered(1)),
                pl.BlockSpec((None, D_MODEL, RET_QK_W), lambda i: (layer, 0, 0),
                             pipeline_mode=pl.Buffered(1))]
    args = [x, g, w_all, w_ret]
    if aliased:
        in_specs += [pl.BlockSpec(memory_space=pl.ANY)] * 2
        args += list(rows_prev)
    return pl.pallas_call(
        kern,
        grid=(t // tm,),
        in_specs=in_specs,
        out_specs=[tok(W_AB), tok(W_CQ), tok(W_CQ), tok(W_CQ), row_spec, row_spec],
        out_shape=[jax.ShapeDtypeStruct((t, W_AB), F32),
                   jax.ShapeDtypeStruct((t, W_CQ), BF16),
                   jax.ShapeDtypeStruct((t, W_CQ), BF16),
                   jax.ShapeDtypeStruct((t, W_CQ), BF16),
                   row_shape, row_shape],
        input_output_aliases={4: 4, 5: 5} if aliased else {},
        scratch_shapes=[pltpu.VMEM((tm, D_MODEL), BF16)],
        compiler_params=_cparams(1),
        name="inproj",
    )(*args)


_AB_RET_Q = (4 * H_A * DK_A) // (H_B * DK_B)
_AB_RET_V = (4 * H_A * DK_A + 2 * H_B * DK_B) // W_B
_LOG_GAMMA = tuple(math.log1p(-(2.0 ** (-5.0 - h))) for h in range(H_B))


def _mixers_kernel(lb_ref, gout_ref, sa0_ref, sb0_ref, cos_ref, sin_ref,
                   aq_ref, af_ref, ai_ref, ag_ref, bq_ref, bk_ref, bv_ref, bg_ref,
                   oa_ref, ob_ref, safin_ref, sbfin_ref, st_scr, row_scr, s_scr,
                   *, first_layer, tc, tr, nb):
    t = pl.program_id(1)
    wq = H_B * DK_B
    wh = H_B * HALF_B

    @pl.when(t == 0)
    def _():
        s_scr[...] = jnp.zeros_like(s_scr)
        for bb in range(nb):
            for h in range(H_A):
                st_scr[bb, h] = sa0_ref[bb, h].T
            for h in range(H_B):
                s_scr[bb, h, h * HALF_B:(h + 1) * HALF_B, :] = sb0_ref[bb, h, :HALF_B, :]
                s_scr[bb, h, wh + h * HALF_B:wh + (h + 1) * HALF_B, :] = sb0_ref[bb, h, HALF_B:, :]

    rows = lax.broadcasted_iota(jnp.int32, (CHUNK, CHUNK), 0)
    cols = lax.broadcasted_iota(jnp.int32, (CHUNK, CHUNK), 1)
    tri = (cols <= rows).astype(BF16)
    sub_rows = lax.broadcasted_iota(jnp.int32, (SUBCHUNK, 1), 0)
    gout = gout_ref[...]
    lb = lb_ref[...]
    n_sub = CHUNK // SUBCHUNK

    def chunk_rows(bb, r0):
        zq = aq_ref[bb, pl.ds(r0, CHUNK), :]
        zf = af_ref[bb, pl.ds(r0, CHUNK), :]
        vi = ai_ref[bb, pl.ds(r0, CHUNK), :]
        zg = ag_ref[bb, pl.ds(r0, CHUNK), :]

        e = jnp.exp(-jnp.abs(zf))
        r = 1.0 / (1.0 + e)
        pos = zf >= 0.0
        if first_layer:
            logf = jnp.minimum(zf, 0.0) - jnp.log(1.0 + e)
        else:
            logf = jnp.log(lb + (1.0 - lb) * jnp.where(pos, r, e * r))
        kk = (1.0 - lb) * jnp.where(pos, e * r, r)
        qs = zq * _sigmoid(zq) * (DK_A ** -0.5)
        gate = zg * _sigmoid(zg)

        g_hi = logf.astype(BF16)
        r1 = logf - g_hi.astype(F32)
        g_mid = r1.astype(BF16)
        g_lo = (r1 - g_mid.astype(F32)).astype(BF16)
        bcum = (_dot(tri, g_hi) + _dot(tri, g_mid) + _dot(tri, g_lo)) * LOG2_E
        row_scr[bb, 0] = bcum
        row_scr[bb, 1] = kk
        row_scr[bb, 2] = vi

        for h in range(H_A):
            sl = slice(h * DK_A, (h + 1) * DK_A)
            b = bcum[:, sl]
            q_h = qs[:, sl]
            k_h = kk[:, sl]
            v_bf = vi[:, sl].astype(BF16)
            st = st_scr[bb, h]
            btot = b[CHUNK - 1:CHUNK, :]

            o_inter = _dot_nt((q_h * jnp.exp2(b)).astype(BF16), st.astype(BF16))
            kdec = (k_h * jnp.exp2(btot - b)).astype(BF16)
            st_scr[bb, h] = st * jnp.exp2(btot) + _dot_tn(v_bf, kdec)

            for i in range(n_sub):
                lo = i * SUBCHUNK
                b_i = b[lo:lo + SUBCHUNK, :]
                q_i = q_h[lo:lo + SUBCHUNK, :]
                o_i = o_inter[lo:lo + SUBCHUNK, :]
                if i > 0:
                    ref_row = b[lo - 1:lo, :]
                    qd = (q_i * jnp.exp2(b_i - ref_row)).astype(BF16)
                    kd = (k_h[:lo, :] * jnp.exp2(ref_row - b[:lo, :])).astype(BF16)
                    a = _dot_nt(qd, kd)
                    o_i = o_i + _dot(a.astype(BF16), v_bf[:lo, :])
                for s in range(SUBCHUNK):
                    sg = lo + s
                    b_s = row_scr[bb, 0, sg:sg + 1, sl]
                    k_s = row_scr[bb, 1, sg:sg + 1, sl]
                    v_s = row_scr[bb, 2, sg:sg + 1, sl]
                    d = jnp.exp2(b_i - b_s)
                    w = jnp.sum(q_i * d * k_s, axis=-1, keepdims=True)
                    w = jnp.where(sub_rows >= s, w, 0.0)
                    o_i = o_i + w * v_s
                y = o_i * _rms_scale(o_i) * gout * gate[lo:lo + SUBCHUNK, sl]
                oa_ref[bb, pl.ds(r0 + lo, SUBCHUNK), sl] = y.astype(BF16)

    lane_head = (lax.broadcasted_iota(jnp.int32, (1, wq), 1) % wh) // HALF_B
    rel = (lax.broadcasted_iota(jnp.int32, (tr, tr), 0)
           - lax.broadcasted_iota(jnp.int32, (tr, tr), 1))
    relf = jnp.maximum(rel, 0).astype(F32)
    tpos = lax.broadcasted_iota(jnp.int32, (tr, 1), 0).astype(F32)

    def ret_rows(r):
        c = cos_ref[pl.ds(r, tr), :]
        s = sin_ref[pl.ds(r, tr), :]

        def rot(x):
            x1, x2 = x[:, :wh], x[:, wh:]
            return jnp.concatenate([x1 * c - x2 * s, x1 * s + x2 * c], axis=1)

        dmats = [jnp.where(rel >= 0, jnp.exp(relf * lg), 0.0) for lg in _LOG_GAMMA]
        inner = [jnp.exp((tpos + 1.0) * lg) for lg in _LOG_GAMMA]
        tail = [jnp.exp((tr - 1.0 - tpos) * lg) for lg in _LOG_GAMMA]
        for bb in range(nb):
            qr = rot(bq_ref[bb, pl.ds(r, tr), :]).astype(BF16)
            kr = rot(bk_ref[bb, pl.ds(r, tr), :]) * (DK_B ** -0.5)
            v = bv_ref[bb, pl.ds(r, tr), :]
            gate_in = bg_ref[bb, pl.ds(r, tr), :]
            for h in range(H_B):
                sl = slice(h * DV_B, (h + 1) * DV_B)
                kh = jnp.where(lane_head == h, kr, 0.0)
                v_bf = v[:, sl].astype(BF16)
                sc = _dot_nt(qr, kh.astype(BF16)) * dmats[h]
                st = s_scr[bb, h]
                o = _dot(sc.astype(BF16), v_bf) + _dot(qr, st.astype(BF16)) * inner[h]
                ktail = (kh * tail[h]).astype(BF16)
                s_scr[bb, h] = math.exp(tr * _LOG_GAMMA[h]) * st + _dot_tn(ktail, v_bf)
                zg = gate_in[:, sl]
                ob_ref[bb, pl.ds(r, tr), sl] = (o * _rms_scale(o) * (zg * _sigmoid(zg))).astype(BF16)

    def group_body(gi, carry):
        r = pl.multiple_of(gi * tr, tr)
        for u in range(tr // CHUNK):
            r0 = pl.multiple_of(gi * tr + u * CHUNK, CHUNK)
            for bb in range(nb):
                chunk_rows(bb, r0)
        ret_rows(r)
        return carry

    lax.fori_loop(0, tc // tr, group_body, 0)

    @pl.when(t == pl.num_programs(1) - 1)
    def _():
        for bb in range(nb):
            for h in range(H_A):
                safin_ref[bb, h] = st_scr[bb, h].T
            for h in range(H_B):
                sbfin_ref[bb, h, :HALF_B, :] = s_scr[bb, h, h * HALF_B:(h + 1) * HALF_B, :]
                sbfin_ref[bb, h, HALF_B:, :] = s_scr[bb, h, wh + h * HALF_B:wh + (h + 1) * HALF_B, :]


def _mixers(ab, s_a, s_b, lb, gout, cos, sin, *, first_layer, tc, tr, nb):
    b, l, _ = ab.shape
    wq = H_B * DK_B
    wh = H_B * HALF_B
    blk = H_A * DK_A
    kern = functools.partial(_mixers_kernel, first_layer=first_layer, tc=tc, tr=tr, nb=nb)

    def col(width, k):
        return pl.BlockSpec((nb, tc, width), lambda bi, ti: (bi, ti, k))

    sa_spec = pl.BlockSpec((nb, H_A, DK_A, DV_A), lambda bi, ti: (bi, 0, 0, 0))
    sb_spec = pl.BlockSpec((nb, H_B, DK_B, DV_B), lambda bi, ti: (bi, 0, 0, 0))
    table = pl.BlockSpec((tc, wh), lambda bi, ti: (ti, 0))
    return pl.pallas_call(
        kern,
        grid=(b // nb, l // tc),
        in_specs=[pl.BlockSpec((1, blk), lambda bi, ti: (0, 0)),
                  pl.BlockSpec((1, DV_A), lambda bi, ti: (0, 0)),
                  sa_spec, sb_spec, table, table,
                  col(blk, 0), col(blk, 1), col(blk, 2), col(blk, 3),
                  col(wq, _AB_RET_Q), col(wq, _AB_RET_Q + 1), col(W_B, _AB_RET_V), col(W_B, _AB_RET_V + 1)],
        out_specs=[col(W_A, 0), col(W_B, 0), sa_spec, sb_spec],
        out_shape=[jax.ShapeDtypeStruct((b, l, W_A), BF16),
                   jax.ShapeDtypeStruct((b, l, W_B), BF16),
                   jax.ShapeDtypeStruct((b, H_A, DK_A, DV_A), F32),
                   jax.ShapeDtypeStruct((b, H_B, DK_B, DV_B), F32)],
        scratch_shapes=[pltpu.VMEM((nb, H_A, DV_A, DK_A), F32),
                        pltpu.VMEM((nb, 3, CHUNK, blk), F32),
                        pltpu.VMEM((nb, H_B, wq, DV_B), F32)],
        compiler_params=_cparams(2),
        name="mixers",
    )(lb, gout, s_a, s_b, cos, sin, ab, ab, ab, ab, ab, ab, ab, ab)


def _diff_out(o1, l1, o2, l2, lam, g, out_scale):
    o = o1 / l1 - lam * (o2 / l2)
    return (o * _rms_scale(o) * g * out_scale).astype(BF16)


def _attn_prompt_kernel(lam_ref, q_ref, k_ref, v_ref, g_ref, o_ref, acc_scr, sa_scr, sb_scr, m_scr, l_scr,
                        *, blk, out_scale):
    qi = pl.program_id(2)
    q = q_ref[...]
    qh = (q[:, :DH_C], q[:, DH_C:])
    rq = lax.broadcasted_iota(jnp.int32, (blk, blk), 0) // CHUNK
    ck = lax.broadcasted_iota(jnp.int32, (blk, blk), 1) // CHUNK
    visible = ck <= rq
    acc_scr[...] = jnp.zeros_like(acc_scr)
    m_scr[...] = jnp.full(m_scr.shape, NEG_BIG, F32)
    l_scr[...] = jnp.zeros_like(l_scr)

    def produce(j, buf):
        off = pl.multiple_of(j * blk, blk)
        kb = k_ref[pl.ds(off, blk), :]
        for half in range(2):
            buf[half] = _dot_nt(qh[half], kb[:, half * DH_C:(half + 1) * DH_C])

    def consume(j, buf, masked):
        off = pl.multiple_of(j * blk, blk)
        vb = v_ref[pl.ds(off, blk), :]
        for half in range(2):
            m_old = m_scr[half]
            s = buf[half]
            if masked:
                s = jnp.where(visible, s, -jnp.inf)
            m_new = jnp.maximum(m_old, jnp.max(s, axis=-1, keepdims=True))
            alpha = jnp.exp2(m_old - m_new)
            p = jnp.exp2(s - jnp.concatenate([m_new] * (blk // LANES), axis=1))
            l_scr[half] = alpha * l_scr[half] + jnp.sum(p, axis=-1, keepdims=True)
            acc_scr[half] = (jnp.concatenate([alpha] * (DV_C // LANES), axis=1) * acc_scr[half]
                             + _dot(p.astype(BF16), vb))
            m_scr[half] = m_new

    produce(0, sa_scr)

    def pair(t, carry):
        produce(2 * t + 1, sb_scr)
        consume(2 * t, sa_scr, False)
        produce(2 * t + 2, sa_scr)
        consume(2 * t + 1, sb_scr, False)
        return carry

    lax.fori_loop(0, qi // 2, pair, 0)

    @pl.when(qi % 2 == 0)
    def _():
        consume(qi, sa_scr, True)

    @pl.when(qi % 2 == 1)
    def _():
        produce(qi, sb_scr)
        consume(qi - 1, sa_scr, False)
        consume(qi, sb_scr, True)

    o_ref[...] = _diff_out(acc_scr[0], l_scr[0][:, :1], acc_scr[1], l_scr[1][:, :1],
                           lam_ref[0, 0], g_ref[...], out_scale)


def _attn_prompt(q, k, v, lam, g, *, blk, out_scale):
    b, l, _ = q.shape
    kern = functools.partial(_attn_prompt_kernel, blk=blk, out_scale=out_scale)
    kv_spec = pl.BlockSpec((None, l, DV_C), lambda bi, hi, qi: (bi, 0, hi))
    return pl.pallas_call(
        kern,
        grid=(b, H_C, l // blk),
        in_specs=[pl.BlockSpec(memory_space=pltpu.SMEM),
                  pl.BlockSpec((None, blk, DV_C), lambda bi, hi, qi: (bi, qi, hi)),
                  kv_spec, kv_spec,
                  pl.BlockSpec((1, DV_C), lambda bi, hi, qi: (0, 0))],
        out_specs=pl.BlockSpec((None, blk, DV_C), lambda bi, hi, qi: (bi, qi, hi)),
        out_shape=jax.ShapeDtypeStruct((b, l, W_C), BF16),
        scratch_shapes=[pltpu.VMEM((2, blk, DV_C), F32),
                        pltpu.VMEM((2, blk, blk), F32),
                        pltpu.VMEM((2, blk, blk), F32),
                        pltpu.VMEM((2, blk, LANES), F32),
                        pltpu.VMEM((2, blk, LANES), F32)],
        compiler_params=_cparams(3),
        name="attn_prompt",
    )(lam, q, k, v, g)


def _attn_sample_kernel(lam_ref, q_ref, kn_ref, vn_ref, k0_ref, k1_ref, v0_ref, v1_ref, g_ref, o_ref,
                        kp_scr, vp_scr, *, out_scale):
    past = kp_scr.shape[0]
    for dst, halves in ((kp_scr, (k0_ref, k1_ref)), (vp_scr, (v0_ref, v1_ref))):
        for c, c_ref in enumerate(halves):
            rows = c_ref.reshape(past * H_C, LANES)
            for h in range(H_C):
                lo = h * DV_C + c * LANES
                dst[:, lo:lo + LANES] = rows[pl.ds(h, past, stride=H_C), :].astype(BF16)

    lam = lam_ref[0, 0]
    g = g_ref[...]
    for h in range(H_C):
        hd = slice(h * DV_C, (h + 1) * DV_C)
        q = q_ref[:, hd]
        kn = kn_ref[:, hd]
        vn = vn_ref[:, hd]
        vp = vp_scr[:, hd]
        outs = []
        for half in range(2):
            hs = slice(half * DH_C, (half + 1) * DH_C)
            s_p = _dot_nt(q[:, hs], kp_scr[:, h * DV_C + half * DH_C:h * DV_C + (half + 1) * DH_C])
            s_n = _dot_nt(q[:, hs], kn[:, hs])
            m = jnp.maximum(jnp.max(s_p, axis=-1, keepdims=True), jnp.max(s_n, axis=-1, keepdims=True))
            p_p = jnp.exp2(s_p - m)
            p_n = jnp.exp2(s_n - m)
            l = jnp.sum(p_p, axis=-1, keepdims=True) + jnp.sum(p_n, axis=-1, keepdims=True)
            outs += [_dot(p_p.astype(BF16), vp) + _dot(p_n.astype(BF16), vn), l]
        o_ref[:, hd] = _diff_out(outs[0], outs[1], outs[2], outs[3], lam, g, out_scale)


def _attn_sample(q, kn, vn, cache_k, cache_v, layer, lam, g, *, out_scale):
    b, n, _ = q.shape
    past = cache_k.shape[2]
    kern = functools.partial(_attn_sample_kernel, out_scale=out_scale)
    new_spec = pl.BlockSpec((None, n, W_C), lambda bi: (bi, 0, 0))

    def past_spec(c):
        return pl.BlockSpec((None, None, past, H_C, LANES), lambda bi: (layer, bi, 0, 0, c))

    return pl.pallas_call(
        kern,
        grid=(b,),
        in_specs=[pl.BlockSpec(memory_space=pltpu.SMEM),
                  new_spec, new_spec, new_spec, past_spec(0), past_spec(1), past_spec(0), past_spec(1),
                  pl.BlockSpec((1, DV_C), lambda bi: (0, 0))],
        out_specs=new_spec,
        out_shape=jax.ShapeDtypeStruct((b, n, W_C), BF16),
        scratch_shapes=[pltpu.VMEM((past, W_C), BF16), pltpu.VMEM((past, W_C), BF16)],
        compiler_params=_cparams(1),
        name="attn_sample",
    )(lam, q, kn, vn, cache_k, cache_k, cache_v, cache_v, g)


def _outproj_kernel(oa_ref, ob_ref, oc_ref, wa_ref, wb_ref, wc_ref, x_ref, gpost_ref, gpre_ref,
                    x1_ref, hm_ref, *, n_split):
    rows = x_ref.shape[0] // n_split
    for r in range(n_split):
        rs = slice(r * rows, (r + 1) * rows)
        y = (_dot(oa_ref[rs, :], wa_ref[...]) + _dot(ob_ref[rs, :], wb_ref[...])
             + _dot(oc_ref[rs, :], wc_ref[...]))
        x1 = x_ref[rs, :] + y * _rms_scale(y) * gpost_ref[...]
        x1_ref[rs, :] = x1
        hm_ref[rs, :] = (x1 * _rms_scale(x1) * gpre_ref[...]).astype(BF16)


def _outproj(oa, ob, oc, w_out, layer, x, gpost, gpre, *, tm):
    t = x.shape[0]
    row = lambda w: pl.BlockSpec((tm, w), lambda i: (i, 0))
    vec = pl.BlockSpec((1, D_MODEL), lambda i: (0, 0))
    return pl.pallas_call(
        functools.partial(_outproj_kernel, n_split=2),
        grid=(t // tm,),
        in_specs=[row(W_A), row(W_B), row(W_C),
                  pl.BlockSpec((None, W_A, D_MODEL), lambda i: (layer, 0, 0)),
                  pl.BlockSpec((None, W_B, D_MODEL), lambda i: (layer, 1, 0)),
                  pl.BlockSpec((None, W_C, D_MODEL), lambda i: (layer, 1, 0)),
                  row(D_MODEL), vec, vec],
        out_specs=[row(D_MODEL), row(D_MODEL)],
        out_shape=[jax.ShapeDtypeStruct((t, D_MODEL), F32),
                   jax.ShapeDtypeStruct((t, D_MODEL), BF16)],
        compiler_params=_cparams(1),
        name="outproj",
    )(oa, ob, oc, w_out, w_out, w_out, x, gpost, gpre)


def _mlp_kernel(hm_ref, wu_ref, wd_ref, x1_ref, g_ref, o_ref, acc_scr):
    f = pl.program_id(1)

    @pl.when(f == 0)
    def _():
        acc_scr[...] = jnp.zeros_like(acc_scr)

    u = jnp.maximum(_dot(hm_ref[...], wu_ref[...]), 0.0)
    acc_scr[...] += _dot((u * u).astype(BF16), wd_ref[...])

    @pl.when(f == pl.num_programs(1) - 1)
    def _():
        y = acc_scr[...]
        o_ref[...] = x1_ref[...] + y * _rms_scale(y) * g_ref[...]


def _mlp(hm, w_up, w_down, layer, x1, g, *, tm, tf):
    t = hm.shape[0]
    row = lambda dt: pl.BlockSpec((tm, D_MODEL), lambda i, f: (i, 0))
    return pl.pallas_call(
        _mlp_kernel,
        grid=(t // tm, D_FF // tf),
        in_specs=[row(BF16),
                  pl.BlockSpec((None, D_MODEL, tf), lambda i, f: (layer, 0, f)),
                  pl.BlockSpec((None, tf, D_MODEL), lambda i, f: (layer, f, 0)),
                  row(F32),
                  pl.BlockSpec((1, D_MODEL), lambda i, f: (0, 0))],
        out_specs=row(F32),
        out_shape=jax.ShapeDtypeStruct((t, D_MODEL), F32),
        scratch_shapes=[pltpu.VMEM((tm, D_MODEL), F32)],
        compiler_params=_cparams(2),
        name="mlp",
    )(hm, w_up, w_down, x1, g)


def _rotary_tables(pos):
    inv = 1.0 / (RET_THETA_BASE ** jnp.linspace(0.0, 1.0, HALF_B, dtype=F32))
    ang = pos[:, None] * inv[None]
    return jnp.tile(jnp.cos(ang), (1, H_B)), jnp.tile(jnp.sin(ang), (1, H_B))


def _prep_w_ret(w):
    def regroup(lo):
        return [w[..., lo + h * DK_B + half * HALF_B:lo + h * DK_B + (half + 1) * HALF_B]
                for half in range(2) for h in range(H_B)]

    wq = H_B * DK_B
    return jnp.concatenate(regroup(RET_QK_LO) + regroup(RET_QK_LO + wq), axis=-1).astype(BF16)


def _pick(n, pref):
    return pref if n % pref == 0 else n


def _layer(x, batch, s_a, s_b, rot, cache, rows_prev, layer, depth, lb, lam, lam_init, wts):
    w_in, w_out, w_up, w_down, g_pre_mix, g_post_mix, g_pre_mlp, g_post_mlp, g_hgrn, g_diff = wts
    t = x.shape[0]
    l = t // batch
    tm = _pick(t, 512)
    ab, q, kbf, vbf, k_rows, v_rows = _inproj(x, g_pre_mix, w_in, rows_prev, layer, depth,
                                              tm=_pick(t, 256), tn=512)
    ab3 = ab.reshape(batch, l, W_AB)
    oa, ob, sa_new, sb_new = _mixers(ab3, s_a, s_b, lb, g_hgrn, rot[0], rot[1], first_layer=(layer == 0),
                                     tc=_pick(l, 512), tr=_pick(l, 256), nb=_pick(batch, 2))
    q3, k3, v3 = (a.reshape(batch, l, W_CQ) for a in (q, kbf, vbf))
    out_scale = 1.0 - lam_init
    if cache is None:
        oc = _attn_prompt(q3, k3, v3, lam, g_diff, blk=512, out_scale=out_scale)
    else:
        oc = _attn_sample(q3, k3, v3, cache[0], cache[1], layer, lam, g_diff, out_scale=out_scale)
    x1, hm = _outproj(oa.reshape(t, W_A), ob.reshape(t, W_B), oc.reshape(t, W_C), w_out, layer, x,
                      g_post_mix, g_pre_mlp, tm=tm)
    x2 = _mlp(hm, w_up, w_down, layer, x1, g_post_mlp, tm=tm, tf=1024)
    return x2, sa_new, sb_new, (k_rows, v_rows)


def kernel(x_prompt, x_sample, state_hgrn, state_ret, cache_k, cache_v, w_in, w_out, w_up, w_down, g_pre_mix, g_post_mix, g_pre_mlp, g_post_mlp, hgrn_lb_logits, g_hgrn_out, lambda_q1, lambda_k1, lambda_q2, lambda_k2, g_diff_out):
    depth = w_in.shape[0]
    b_p, seq_p, _ = x_prompt.shape
    b_s, n_new, _ = x_sample.shape
    past_len = cache_k.shape[2]

    rot_p = _rotary_tables(jnp.arange(seq_p, dtype=F32))
    rot_s = _rotary_tables(past_len + jnp.arange(n_new, dtype=F32))
    lb_soft = jax.nn.softmax(hgrn_lb_logits.astype(F32), axis=0)
    lower_bounds = jnp.cumsum(lb_soft, axis=0) - lb_soft[0]
    cache = (cache_k, cache_v)
    sa0 = jnp.zeros((b_p, H_A, DK_A, DV_A), F32)
    sb0 = jnp.zeros((b_p, H_B, DK_B, DV_B), F32)

    hp = x_prompt.reshape(b_p * seq_p, D_MODEL)
    hs = x_sample.reshape(b_s * n_new, D_MODEL)
    w_in_bf, w_out_bf, w_up_bf, w_down_bf = ((w_in.astype(BF16), _prep_w_ret(w_in)), w_out.astype(BF16),
                                             w_up.astype(BF16), w_down.astype(BF16))
    states_p, states_s = [], []
    rows_p = rows_s = None
    for l in range(depth):
        lam_init = 0.8 - 0.6 * math.exp(-0.3 * l)
        lam = (jnp.exp(jnp.sum(lambda_q1[l] * lambda_k1[l])) - jnp.exp(jnp.sum(lambda_q2[l] * lambda_k2[l]))
               + lam_init).astype(F32).reshape(1, 1)
        vec = lambda a: a[l].reshape(1, -1)
        wts = (w_in_bf, w_out_bf, w_up_bf, w_down_bf,
               vec(g_pre_mix), vec(g_post_mix), vec(g_pre_mlp), vec(g_post_mlp), vec(g_hgrn_out),
               vec(g_diff_out))
        lb = lower_bounds[l].reshape(1, -1)
        hp, sa_p, sb_p, rows_p = _layer(hp, b_p, sa0, sb0, rot_p, None, rows_p, l, depth, lb, lam,
                                        lam_init, wts)
        hs, sa_s, sb_s, rows_s = _layer(hs, b_s, state_hgrn[l], state_ret[l], rot_s, cache, rows_s, l, depth,
                                        lb, lam, lam_init, wts)
        states_p.append((sa_p, sb_p))
        states_s.append((sa_s, sb_s))

    stack = lambda outs, i: jnp.stack([o[i] for o in outs])
    rows = lambda a, b, n: a.reshape(depth, b, n, H_C, DV_C)
    return (hp.reshape(b_p, seq_p, D_MODEL), hs.reshape(b_s, n_new, D_MODEL),
            stack(states_p, 0), stack(states_s, 0), stack(states_p, 1), stack(states_s, 1),
            rows(rows_p[0], b_p, seq_p), rows(rows_s[0], b_s, n_new),
            rows(rows_p[1], b_p, seq_p), rows(rows_s[1], b_s, n_new))
```

```python
import functools
import math

import jax
import jax.numpy as jnp
from jax import lax
from jax.experimental import pallas as pl
from jax.experimental.pallas import tpu as pltpu

F32 = jnp.float32
BF16 = jnp.bfloat16

D_MODEL = 2048
CHUNK = 64
EPS = 1e-6
H_A, DK_A, DV_A = 4, 128, 128
H_B, DK_B, DV_B = 4, 64, 128
RET_THETA_BASE = 10000.0
H_C, DH_C, DV_C = 4, 128, 256
D_FF = 4 * D_MODEL
W_A = H_A * DV_A
W_B = H_B * DV_B
W_C = H_C * DV_C
W_AB = 4 * H_A * DK_A + 2 * H_B * DK_B + 2 * H_B * DV_B
W_CQ = H_C * 2 * DH_C
IN_WIDTH = W_AB + 3 * W_CQ
HALF_B = DK_B // 2
RET_QK_LO = 4 * H_A * DK_A
RET_QK_W = 2 * H_B * DK_B

VMEM_LIMIT_BYTES = 56 * 1024 * 1024
SUBCHUNK = 16
NEG_BIG = -1e30
LOG2_E = 1.4426950408889634
LANES = 128


def _cparams(n_axes):
    return pltpu.CompilerParams(dimension_semantics=("arbitrary",) * n_axes,
                                vmem_limit_bytes=VMEM_LIMIT_BYTES)


def _rms_scale(x):
    return lax.rsqrt(jnp.mean(x * x, axis=-1, keepdims=True) + EPS)


def _sigmoid(x):
    return 1.0 / (1.0 + jnp.exp(-x))


def _dot(a, b):
    return jnp.dot(a, b, preferred_element_type=F32)


def _dot_nt(a, b):
    return lax.dot_general(a, b, (((1,), (1,)), ((), ())), preferred_element_type=F32)


def _dot_tn(a, b):
    return lax.dot_general(a, b, (((0,), (0,)), ((), ())), preferred_element_type=F32)


def _inproj_kernel(*refs, q_scale, aliased, tn):
    x_ref, g_ref, w_ref, wret_ref = refs[:4]
    ab_ref, q_ref, kbf_ref, vbf_ref, krow_ref, vrow_ref, h_scr = refs[6 if aliased else 4:]
    x = x_ref[...]
    h_scr[...] = (x * _rms_scale(x) * g_ref[...]).astype(BF16)
    heads_per_group = tn // DV_C

    def cols(lo):
        w = wret_ref[...] if lo == RET_QK_LO else w_ref[:, lo:lo + tn]
        return _dot(h_scr[...], w)

    for c in range(W_AB // tn):
        ab_ref[:, c * tn:(c + 1) * tn] = cols(c * tn)
    for c in range(W_CQ // tn):
        q_ref[:, c * tn:(c + 1) * tn] = (cols(W_AB + c * tn) * q_scale).astype(BF16)
    for group, (bf_ref, row_ref) in enumerate(((kbf_ref, krow_ref), (vbf_ref, vrow_ref))):
        for c in range(W_CQ // tn):
            acc = cols(W_AB + (1 + group) * W_CQ + c * tn)
            bf_ref[:, c * tn:(c + 1) * tn] = acc.astype(BF16)
            for hh in range(heads_per_group):
                row_ref[:, c * heads_per_group + hh, :] = acc[:, hh * DV_C:(hh + 1) * DV_C]


def _inproj(x, g, w, rows_prev, layer, depth, *, tm, tn):
    t = x.shape[0]
    w_all, w_ret = w
    assert tn == RET_QK_W and RET_QK_LO % tn == 0
    aliased = rows_prev is not None
    kern = functools.partial(_inproj_kernel, q_scale=DH_C ** -0.5 * LOG2_E, aliased=aliased, tn=tn)
    row_spec = pl.BlockSpec((None, tm, H_C, DV_C), lambda i: (layer, i, 0, 0))
    row_shape = jax.ShapeDtypeStruct((depth, t, H_C, DV_C), F32)
    tok = lambda width: pl.BlockSpec((tm, width), lambda i: (i, 0))
    in_specs = [tok(D_MODEL),
                pl.BlockSpec((1, D_MODEL), lambda i: (0, 0)),
                pl.BlockSpec((None, D_MODEL, IN_WIDTH), lambda i: (layer, 0, 0),
                             pipeline_mode=pl.Buffered(1)),
                pl.BlockSpec((None, D_MODEL, RET_QK_W), lambda i: (layer, 0, 0),
                             pipeline_mode=pl.Buffered(1))]
    args = [x, g, w_all, w_ret]
    if aliased:
        in_specs += [pl.BlockSpec(memory_space=pl.ANY)] * 2
        args += list(rows_prev)
    return pl.pallas_call(
        kern,
        grid=(t // tm,),
        in_specs=in_specs,
        out_specs=[tok(W_AB), tok(W_CQ), tok(W_CQ), tok(W_CQ), row_spec, row_spec],
        out_shape=[jax.ShapeDtypeStruct((t, W_AB), F32),
                   jax.ShapeDtypeStruct((t, W_CQ), BF16),
                   jax.ShapeDtypeStruct((t, W_CQ), BF16),
                   jax.ShapeDtypeStruct((t, W_CQ), BF16),
                   row_shape, row_shape],
        input_output_aliases={4: 4, 5: 5} if aliased else {},
        scratch_shapes=[pltpu.VMEM((tm, D_MODEL), BF16)],
        compiler_params=_cparams(1),
        name="inproj",
    )(*args)


_AB_RET_Q = (4 * H_A * DK_A) // (H_B * DK_B)
_AB_RET_V = (4 * H_A * DK_A + 2 * H_B * DK_B) // W_B
_LOG_GAMMA = tuple(math.log1p(-(2.0 ** (-5.0 - h))) for h in range(H_B))


def _mixers_kernel(lb_ref, gout_ref, sa0_ref, sb0_ref, cos_ref, sin_ref,
                   aq_ref, af_ref, ai_ref, ag_ref, bq_ref, bk_ref, bv_ref, bg_ref,
                   oa_ref, ob_ref, safin_ref, sbfin_ref, st_scr, row_scr, s_scr,
                   *, first_layer, tc, tr, nb):
    t = pl.program_id(1)
    wq = H_B * DK_B
    wh = H_B * HALF_B

    @pl.when(t == 0)
    def _():
        s_scr[...] = jnp.zeros_like(s_scr)
        for bb in range(nb):
            for h in range(H_A):
                st_scr[bb, h] = sa0_ref[bb, h].T
            for h in range(H_B):
                s_scr[bb, h, h * HALF_B:(h + 1) * HALF_B, :] = sb0_ref[bb, h, :HALF_B, :]
                s_scr[bb, h, wh + h * HALF_B:wh + (h + 1) * HALF_B, :] = sb0_ref[bb, h, HALF_B:, :]

    rows = lax.broadcasted_iota(jnp.int32, (CHUNK, CHUNK), 0)
    cols = lax.broadcasted_iota(jnp.int32, (CHUNK, CHUNK), 1)
    tri = (cols <= rows).astype(BF16)
    sub_rows = lax.broadcasted_iota(jnp.int32, (SUBCHUNK, 1), 0)
    gout = gout_ref[...]
    lb = lb_ref[...]
    n_sub = CHUNK // SUBCHUNK

    def chunk_rows(bb, r0):
        zq = aq_ref[bb, pl.ds(r0, CHUNK), :]
        zf = af_ref[bb, pl.ds(r0, CHUNK), :]
        vi = ai_ref[bb, pl.ds(r0, CHUNK), :]
        zg = ag_ref[bb, pl.ds(r0, CHUNK), :]

        e = jnp.exp(-jnp.abs(zf))
        r = 1.0 / (1.0 + e)
        pos = zf >= 0.0
        if first_layer:
            logf = jnp.minimum(zf, 0.0) - jnp.log(1.0 + e)
        else:
            logf = jnp.log(lb + (1.0 - lb) * jnp.where(pos, r, e * r))
        kk = (1.0 - lb) * jnp.where(pos, e * r, r)
        qs = zq * _sigmoid(zq) * (DK_A ** -0.5)
        gate = zg * _sigmoid(zg)

        g_hi = logf.astype(BF16)
        r1 = logf - g_hi.astype(F32)
        g_mid = r1.astype(BF16)
        g_lo = (r1 - g_mid.astype(F32)).astype(BF16)
        bcum = (_dot(tri, g_hi) + _dot(tri, g_mid) + _dot(tri, g_lo)) * LOG2_E
        row_scr[bb, 0] = bcum
        row_scr[bb, 1] = kk
        row_scr[bb, 2] = vi

        for h in range(H_A):
            sl = slice(h * DK_A, (h + 1) * DK_A)
            b = bcum[:, sl]
            q_h = qs[:, sl]
            k_h = kk[:, sl]
            v_bf = vi[:, sl].astype(BF16)
            st = st_scr[bb, h]
            btot = b[CHUNK - 1:CHUNK, :]

            o_inter = _dot_nt((q_h * jnp.exp2(b)).astype(BF16), st.astype(BF16))
            kdec = (k_h * jnp.exp2(btot - b)).astype(BF16)
            st_scr[bb, h] = st * jnp.exp2(btot) + _dot_tn(v_bf, kdec)

            for i in range(n_sub):
                lo = i * SUBCHUNK
                b_i = b[lo:lo + SUBCHUNK, :]
                q_i = q_h[lo:lo + SUBCHUNK, :]
                o_i = o_inter[lo:lo + SUBCHUNK, :]
                if i > 0:
                    ref_row = b[lo - 1:lo, :]
                    qd = (q_i * jnp.exp2(b_i - ref_row)).astype(BF16)
                    kd = (k_h[:lo, :] * jnp.exp2(ref_row - b[:lo, :])).astype(BF16)
                    a = _dot_nt(qd, kd)
                    o_i = o_i + _dot(a.astype(BF16), v_bf[:lo, :])
                for s in range(SUBCHUNK):
                    sg = lo + s
                    b_s = row_scr[bb, 0, sg:sg + 1, sl]
                    k_s = row_scr[bb, 1, sg:sg + 1, sl]
                    v_s = row_scr[bb, 2, sg:sg + 1, sl]
                    d = jnp.exp2(b_i - b_s)
                    w = jnp.sum(q_i * d * k_s, axis=-1, keepdims=True)
                    w = jnp.where(sub_rows >= s, w, 0.0)
                    o_i = o_i + w * v_s
                y = o_i * _rms_scale(o_i) * gout * gate[lo:lo + SUBCHUNK, sl]
                oa_ref[bb, pl.ds(r0 + lo, SUBCHUNK), sl] = y.astype(BF16)

    lane_head = (lax.broadcasted_iota(jnp.int32, (1, wq), 1) % wh) // HALF_B
    rel = (lax.broadcasted_iota(jnp.int32, (tr, tr), 0)
           - lax.broadcasted_iota(jnp.int32, (tr, tr), 1))
    relf = jnp.maximum(rel, 0).astype(F32)
    tpos = lax.broadcasted_iota(jnp.int32, (tr, 1), 0).astype(F32)

    def ret_rows(r):
        c = cos_ref[pl.ds(r, tr), :]
        s = sin_ref[pl.ds(r, tr), :]

        def rot(x):
            x1, x2 = x[:, :wh], x[:, wh:]
            return jnp.concatenate([x1 * c - x2 * s, x1 * s + x2 * c], axis=1)

        dmats = [jnp.where(rel >= 0, jnp.exp(relf * lg), 0.0) for lg in _LOG_GAMMA]
        inner = [jnp.exp((tpos + 1.0) * lg) for lg in _LOG_GAMMA]
        tail = [jnp.exp((tr - 1.0 - tpos) * lg) for lg in _LOG_GAMMA]
        for bb in range(nb):
            qr = rot(bq_ref[bb, pl.ds(r, tr), :]).astype(BF16)
            kr = rot(bk_ref[bb, pl.ds(r, tr), :]) * (DK_B ** -0.5)
            v = bv_ref[bb, pl.ds(r, tr), :]
            gate_in = bg_ref[bb, pl.ds(r, tr), :]
            for h in range(H_B):
                sl = slice(h * DV_B, (h + 1) * DV_B)
                kh = jnp.where(lane_head == h, kr, 0.0)
                v_bf = v[:, sl].astype(BF16)
                sc = _dot_nt(qr, kh.astype(BF16)) * dmats[h]
                st = s_scr[bb, h]
                o = _dot(sc.astype(BF16), v_bf) + _dot(qr, st.astype(BF16)) * inner[h]
                ktail = (kh * tail[h]).astype(BF16)
                s_scr[bb, h] = math.exp(tr * _LOG_GAMMA[h]) * st + _dot_tn(ktail, v_bf)
                zg = gate_in[:, sl]
                ob_ref[bb, pl.ds(r, tr), sl] = (o * _rms_scale(o) * (zg * _sigmoid(zg))).astype(BF16)

    def group_body(gi, carry):
        r = pl.multiple_of(gi * tr, tr)
        for u in range(tr // CHUNK):
            r0 = pl.multiple_of(gi * tr + u * CHUNK, CHUNK)
            for bb in range(nb):
                chunk_rows(bb, r0)
        ret_rows(r)
        return carry

    lax.fori_loop(0, tc // tr, group_body, 0)

    @pl.when(t == pl.num_programs(1) - 1)
    def _():
        for bb in range(nb):
            for h in range(H_A):
                safin_ref[bb, h] = st_scr[bb, h].T
            for h in range(H_B):
                sbfin_ref[bb, h, :HALF_B, :] = s_scr[bb, h, h * HALF_B:(h + 1) * HALF_B, :]
                sbfin_ref[bb, h, HALF_B:, :] = s_scr[bb, h, wh + h * HALF_B:wh + (h + 1) * HALF_B, :]


def _mixers(ab, s_a, s_b, lb, gout, cos, sin, *, first_layer, tc, tr, nb):
    b, l, _ = ab.shape
    wq = H_B * DK_B
    wh = H_B * HALF_B
    blk = H_A * DK_A
    kern = functools.partial(_mixers_kernel, first_layer=first_layer, tc=tc, tr=tr, nb=nb)

    def col(width, k):
        return pl.BlockSpec((nb, tc, width), lambda bi, ti: (bi, ti, k))

    sa_spec = pl.BlockSpec((nb, H_A, DK_A, DV_A), lambda bi, ti: (bi, 0, 0, 0))
    sb_spec = pl.BlockSpec((nb, H_B, DK_B, DV_B), lambda bi, ti: (bi, 0, 0, 0))
    table = pl.BlockSpec((tc, wh), lambda bi, ti: (ti, 0))
    return pl.pallas_call(
        kern,
        grid=(b // nb, l // tc),
        in_specs=[pl.BlockSpec((1, blk), lambda bi, ti: (0, 0)),
                  pl.BlockSpec((1, DV_A), lambda bi, ti: (0, 0)),
                  sa_spec, sb_spec, table, table,
                  col(blk, 0), col(blk, 1), col(blk, 2), col(blk, 3),
                  col(wq, _AB_RET_Q), col(wq, _AB_RET_Q + 1), col(W_B, _AB_RET_V), col(W_B, _AB_RET_V + 1)],
        out_specs=[col(W_A, 0), col(W_B, 0), sa_spec, sb_spec],
        out_shape=[jax.ShapeDtypeStruct((b, l, W_A), BF16),
                   jax.ShapeDtypeStruct((b, l, W_B), BF16),
                   jax.ShapeDtypeStruct((b, H_A, DK_A, DV_A), F32),
                   jax.ShapeDtypeStruct((b, H_B, DK_B, DV_B), F32)],
        scratch_shapes=[pltpu.VMEM((nb, H_A, DV_A, DK_A), F32),
                        pltpu.VMEM((nb, 3, CHUNK, blk), F32),
                        pltpu.VMEM((nb, H_B, wq, DV_B), F32)],
        compiler_params=_cparams(2),
        name="mixers",
    )(lb, gout, s_a, s_b, cos, sin, ab, ab, ab, ab, ab, ab, ab, ab)


def _diff_out(o1, l1, o2, l2, lam, g, out_scale):
    o = o1 / l1 - lam * (o2 / l2)
    return (o * _rms_scale(o) * g * out_scale).astype(BF16)


def _attn_prompt_kernel(lam_ref, q_ref, k_ref, v_ref, g_ref, o_ref, acc_scr, sa_scr, sb_scr, m_scr, l_scr,
                        *, blk, out_scale):
    qi = pl.program_id(2)
    q = q_ref[...]
    qh = (q[:, :DH_C], q[:, DH_C:])
    acc_scr[...] = jnp.zeros_like(acc_scr)
    m_scr[...] = jnp.full(m_scr.shape, NEG_BIG, F32)
    l_scr[...] = jnp.zeros_like(l_scr)

    def produce(j, buf):
        off = pl.multiple_of(j * blk, blk)
        kb = k_ref[pl.ds(off, blk), :]
        for half in range(2):
            buf[half] = _dot_nt(qh[half], kb[:, half * DH_C:(half + 1) * DH_C])

    def consume(j, buf, masked):
        off = pl.multiple_of(j * blk, blk)
        vb = v_ref[pl.ds(off, blk), :]
        if masked:
            rq = lax.broadcasted_iota(jnp.int32, (blk, blk), 0) // CHUNK
            ck = lax.broadcasted_iota(jnp.int32, (blk, blk), 1) // CHUNK
            visible = ck <= rq
        for half in range(2):
            m_old = m_scr[half]
            s = buf[half]
            if masked:
                s = jnp.where(visible, s, -jnp.inf)
            m_new = jnp.maximum(m_old, jnp.max(s, axis=-1, keepdims=True))
            alpha = jnp.exp2(m_old - m_new)
            p = jnp.exp2(s - jnp.concatenate([m_new] * (blk // LANES), axis=1))
            l_scr[half] = alpha * l_scr[half] + jnp.sum(p, axis=-1, keepdims=True)
            acc_scr[half] = (jnp.concatenate([alpha] * (DV_C // LANES), axis=1) * acc_scr[half]
                             + _dot(p.astype(BF16), vb))
            m_scr[half] = m_new

    produce(0, sa_scr)

    def pair(t, carry):
        produce(2 * t + 1, sb_scr)
        consume(2 * t, sa_scr, False)
        produce(2 * t + 2, sa_scr)
        consume(2 * t + 1, sb_scr, False)
        return carry

    lax.fori_loop(0, qi // 2, pair, 0)

    @pl.when(qi % 2 == 0)
    def _():
        consume(qi, sa_scr, True)

    @pl.when(qi % 2 == 1)
    def _():
        produce(qi, sb_scr)
        consume(qi - 1, sa_scr, False)
        consume(qi, sb_scr, True)

    o_ref[...] = _diff_out(acc_scr[0], l_scr[0][:, :1], acc_scr[1], l_scr[1][:, :1],
                           lam_ref[0, 0], g_ref[...], out_scale)


def _attn_prompt(q, k, v, lam, g, *, blk, out_scale):
    b, l, _ = q.shape
    kern = functools.partial(_attn_prompt_kernel, blk=blk, out_scale=out_scale)
    kv_spec = pl.BlockSpec((None, l, DV_C), lambda bi, hi, qi: (bi, 0, hi))
    return pl.pallas_call(
        kern,
        grid=(b, H_C, l // blk),
        in_specs=[pl.BlockSpec(memory_space=pltpu.SMEM),
                  pl.BlockSpec((None, blk, DV_C), lambda bi, hi, qi: (bi, qi, hi)),
                  kv_spec, kv_spec,
                  pl.BlockSpec((1, DV_C), lambda bi, hi, qi: (0, 0))],
        out_specs=pl.BlockSpec((None, blk, DV_C), lambda bi, hi, qi: (bi, qi, hi)),
        out_shape=jax.ShapeDtypeStruct((b, l, W_C), BF16),
        scratch_shapes=[pltpu.VMEM((2, blk, DV_C), F32),
                        pltpu.VMEM((2, blk, blk), F32),
                        pltpu.VMEM((2, blk, blk), F32),
                        pltpu.VMEM((2, blk, LANES), F32),
                        pltpu.VMEM((2, blk, LANES), F32)],
        compiler_params=_cparams(3),
        name="attn_prompt",
    )(lam, q, k, v, g)


def _attn_sample_kernel(lam_ref, q_ref, kn_ref, vn_ref, k0_ref, k1_ref, v0_ref, v1_ref, g_ref, o_ref,
                        kp_scr, vp_scr, *, out_scale):
    past = kp_scr.shape[0]
    for dst, halves in ((kp_scr, (k0_ref, k1_ref)), (vp_scr, (v0_ref, v1_ref))):
        for c, c_ref in enumerate(halves):
            rows = c_ref.reshape(past * H_C, LANES)
            for h in range(H_C):
                lo = h * DV_C + c * LANES
                dst[:, lo:lo + LANES] = rows[pl.ds(h, past, stride=H_C), :].astype(BF16)

    lam = lam_ref[0, 0]
    g = g_ref[...]
    for h in range(H_C):
        hd = slice(h * DV_C, (h + 1) * DV_C)
        q = q_ref[:, hd]
        kn = kn_ref[:, hd]
        vn = vn_ref[:, hd]
        vp = vp_scr[:, hd]
        outs = []
        for half in range(2):
            hs = slice(half * DH_C, (half + 1) * DH_C)
            s_p = _dot_nt(q[:, hs], kp_scr[:, h * DV_C + half * DH_C:h * DV_C + (half + 1) * DH_C])
            s_n = _dot_nt(q[:, hs], kn[:, hs])
            m = jnp.maximum(jnp.max(s_p, axis=-1, keepdims=True), jnp.max(s_n, axis=-1, keepdims=True))
            p_p = jnp.exp2(s_p - m)
            p_n = jnp.exp2(s_n - m)
            l = jnp.sum(p_p, axis=-1, keepdims=True) + jnp.sum(p_n, axis=-1, keepdims=True)
            outs += [_dot(p_p.astype(BF16), vp) + _dot(p_n.astype(BF16), vn), l]
        o_ref[:, hd] = _diff_out(outs[0], outs[1], outs[2], outs[3], lam, g, out_scale)


def _attn_sample(q, kn, vn, cache_k, cache_v, layer, lam, g, *, out_scale):
    b, n, _ = q.shape
    past = cache_k.shape[2]
    kern = functools.partial(_attn_sample_kernel, out_scale=out_scale)
    new_spec = pl.BlockSpec((None, n, W_C), lambda bi: (bi, 0, 0))

    def past_spec(c):
        return pl.BlockSpec((None, None, past, H_C, LANES), lambda bi: (layer, bi, 0, 0, c))

    return pl.pallas_call(
        kern,
        grid=(b,),
        in_specs=[pl.BlockSpec(memory_space=pltpu.SMEM),
                  new_spec, new_spec, new_spec, past_spec(0), past_spec(1), past_spec(0), past_spec(1),
                  pl.BlockSpec((1, DV_C), lambda bi: (0, 0))],
        out_specs=new_spec,
        out_shape=jax.ShapeDtypeStruct((b, n, W_C), BF16),
        scratch_shapes=[pltpu.VMEM((past, W_C), BF16), pltpu.VMEM((past, W_C), BF16)],
        compiler_params=_cparams(1),
        name="attn_sample",
    )(lam, q, kn, vn, cache_k, cache_k, cache_v, cache_v, g)


def _outproj_kernel(oa_ref, ob_ref, oc_ref, wa_ref, wb_ref, wc_ref, x_ref, gpost_ref, gpre_ref,
                    x1_ref, hm_ref, *, n_split):
    rows = x_ref.shape[0] // n_split
    for r in range(n_split):
        rs = slice(r * rows, (r + 1) * rows)
        y = (_dot(oa_ref[rs, :], wa_ref[...]) + _dot(ob_ref[rs, :], wb_ref[...])
             + _dot(oc_ref[rs, :], wc_ref[...]))
        x1 = x_ref[rs, :] + y * _rms_scale(y) * gpost_ref[...]
        x1_ref[rs, :] = x1
        hm_ref[rs, :] = (x1 * _rms_scale(x1) * gpre_ref[...]).astype(BF16)


def _outproj(oa, ob, oc, w_out, layer, x, gpost, gpre, *, tm):
    t = x.shape[0]
    row = lambda w: pl.BlockSpec((tm, w), lambda i: (i, 0))
    vec = pl.BlockSpec((1, D_MODEL), lambda i: (0, 0))
    return pl.pallas_call(
        functools.partial(_outproj_kernel, n_split=4),
        grid=(t // tm,),
        in_specs=[row(W_A), row(W_B), row(W_C),
                  pl.BlockSpec((None, W_A, D_MODEL), lambda i: (layer, 0, 0)),
                  pl.BlockSpec((None, W_B, D_MODEL), lambda i: (layer, 1, 0)),
                  pl.BlockSpec((None, W_C, D_MODEL), lambda i: (layer, 1, 0)),
                  row(D_MODEL), vec, vec],
        out_specs=[row(D_MODEL), row(D_MODEL)],
        out_shape=[jax.ShapeDtypeStruct((t, D_MODEL), F32),
                   jax.ShapeDtypeStruct((t, D_MODEL), BF16)],
        compiler_params=_cparams(1),
        name="outproj",
    )(oa, ob, oc, w_out, w_out, w_out, x, gpost, gpre)


def _mlp_kernel(hm_ref, wu_ref, wd_ref, x1_ref, g_ref, o_ref, acc_scr):
    f = pl.program_id(1)

    @pl.when(f == 0)
    def _():
        acc_scr[...] = jnp.zeros_like(acc_scr)

    u = jnp.maximum(_dot(hm_ref[...], wu_ref[...]), 0.0)
    acc_scr[...] += _dot((u * u).astype(BF16), wd_ref[...])

    @pl.when(f == pl.num_programs(1) - 1)
    def _():
        y = acc_scr[...]
        o_ref[...] = x1_ref[...] + y * _rms_scale(y) * g_ref[...]


def _mlp(hm, w_up, w_down, layer, x1, g, *, tm, tf):
    t = hm.shape[0]
    row = lambda dt: pl.BlockSpec((tm, D_MODEL), lambda i, f: (i, 0))
    return pl.pallas_call(
        _mlp_kernel,
        grid=(t // tm, D_FF // tf),
        in_specs=[row(BF16),
                  pl.BlockSpec((None, D_MODEL, tf), lambda i, f: (layer, 0, f)),
                  pl.BlockSpec((None, tf, D_MODEL), lambda i, f: (layer, f, 0)),
                  row(F32),
                  pl.BlockSpec((1, D_MODEL), lambda i, f: (0, 0))],
        out_specs=row(F32),
        out_shape=jax.ShapeDtypeStruct((t, D_MODEL), F32),
        scratch_shapes=[pltpu.VMEM((tm, D_MODEL), F32)],
        compiler_params=_cparams(2),
        name="mlp",
    )(hm, w_up, w_down, x1, g)


def _rotary_tables(pos):
    inv = 1.0 / (RET_THETA_BASE ** jnp.linspace(0.0, 1.0, HALF_B, dtype=F32))
    ang = pos[:, None] * inv[None]
    return jnp.tile(jnp.cos(ang), (1, H_B)), jnp.tile(jnp.sin(ang), (1, H_B))


def _prep_w_ret(w):
    def regroup(lo):
        return [w[..., lo + h * DK_B + half * HALF_B:lo + h * DK_B + (half + 1) * HALF_B]
                for half in range(2) for h in range(H_B)]

    wq = H_B * DK_B
    return jnp.concatenate(regroup(RET_QK_LO) + regroup(RET_QK_LO + wq), axis=-1).astype(BF16)


def _pick(n, pref):
    return pref if n % pref == 0 else n


def _layer(x, batch, s_a, s_b, rot, cache, rows_prev, layer, depth, lb, lam, lam_init, wts):
    w_in, w_out, w_up, w_down, g_pre_mix, g_post_mix, g_pre_mlp, g_post_mlp, g_hgrn, g_diff = wts
    t = x.shape[0]
    l = t // batch
    tm = _pick(t, 512)
    ab, q, kbf, vbf, k_rows, v_rows = _inproj(x, g_pre_mix, w_in, rows_prev, layer, depth,
                                              tm=_pick(t, 256), tn=512)
    ab3 = ab.reshape(batch, l, W_AB)
    oa, ob, sa_new, sb_new = _mixers(ab3, s_a, s_b, lb, g_hgrn, rot[0], rot[1], first_layer=(layer == 0),
                                     tc=_pick(l, 512), tr=_pick(l, 256), nb=_pick(batch, 2))
    q3, k3, v3 = (a.reshape(batch, l, W_CQ) for a in (q, kbf, vbf))
    out_scale = 1.0 - lam_init
    if cache is None:
        oc = _attn_prompt(q3, k3, v3, lam, g_diff, blk=512, out_scale=out_scale)
    else:
        oc = _attn_sample(q3, k3, v3, cache[0], cache[1], layer, lam, g_diff, out_scale=out_scale)
    x1, hm = _outproj(oa.reshape(t, W_A), ob.reshape(t, W_B), oc.reshape(t, W_C), w_out, layer, x,
                      g_post_mix, g_pre_mlp, tm=tm)
    x2 = _mlp(hm, w_up, w_down, layer, x1, g_post_mlp, tm=tm, tf=1024)
    return x2, sa_new, sb_new, (k_rows, v_rows)


def kernel(x_prompt, x_sample, state_hgrn, state_ret, cache_k, cache_v, w_in, w_out, w_up, w_down, g_pre_mix, g_post_mix, g_pre_mlp, g_post_mlp, hgrn_lb_logits, g_hgrn_out, lambda_q1, lambda_k1, lambda_q2, lambda_k2, g_diff_out):
    depth = w_in.shape[0]
    b_p, seq_p, _ = x_prompt.shape
    b_s, n_new, _ = x_sample.shape
    past_len = cache_k.shape[2]

    rot_p = _rotary_tables(jnp.arange(seq_p, dtype=F32))
    rot_s = _rotary_tables(past_len + jnp.arange(n_new, dtype=F32))
    lb_soft = jax.nn.softmax(hgrn_lb_logits.astype(F32), axis=0)
    lower_bounds = jnp.cumsum(lb_soft, axis=0) - lb_soft[0]
    cache = (cache_k, cache_v)
    sa0 = jnp.zeros((b_p, H_A, DK_A, DV_A), F32)
    sb0 = jnp.zeros((b_p, H_B, DK_B, DV_B), F32)

    hp = x_prompt.reshape(b_p * seq_p, D_MODEL)
    hs = x_sample.reshape(b_s * n_new, D_MODEL)
    w_in_bf, w_out_bf, w_up_bf, w_down_bf = ((w_in.astype(BF16), _prep_w_ret(w_in)), w_out.astype(BF16),
                                             w_up.astype(BF16), w_down.astype(BF16))
    states_p, states_s = [], []
    rows_p = rows_s = None
    for l in range(depth):
        lam_init = 0.8 - 0.6 * math.exp(-0.3 * l)
        lam = (jnp.exp(jnp.sum(lambda_q1[l] * lambda_k1[l])) - jnp.exp(jnp.sum(lambda_q2[l] * lambda_k2[l]))
               + lam_init).astype(F32).reshape(1, 1)
        vec = lambda a: a[l].reshape(1, -1)
        wts = (w_in_bf, w_out_bf, w_up_bf, w_down_bf,
               vec(g_pre_mix), vec(g_post_mix), vec(g_pre_mlp), vec(g_post_mlp), vec(g_hgrn_out),
               vec(g_diff_out))
        lb = lower_bounds[l].reshape(1, -1)
        hp, sa_p, sb_p, rows_p = _layer(hp, b_p, sa0, sb0, rot_p, None, rows_p, l, depth, lb, lam,
                                        lam_init, wts)
        hs, sa_s, sb_s, rows_s = _layer(hs, b_s, state_hgrn[l], state_ret[l], rot_s, cache, rows_s, l, depth,
                                        lb, lam, lam_init, wts)
        states_p.append((sa_p, sb_p))
        states_s.append((sa_s, sb_s))

    stack = lambda outs, i: jnp.stack([o[i] for o in outs])
    rows = lambda a, b, n: a.reshape(depth, b, n, H_C, DV_C)
    return (hp.reshape(b_p, seq_p, D_MODEL), hs.reshape(b_s, n_new, D_MODEL),
            stack(states_p, 0), stack(states_s, 0), stack(states_p, 1), stack(states_s, 1),
            rows(rows_p[0], b_p, seq_p), rows(rows_s[0], b_s, n_new),
            rows(rows_p[1], b_p, seq_p), rows(rows_s[1], b_s, n_new))
```
